```python
import math
import jax
import jax.numpy as jnp
from jax import lax
import numpy as np


D_MODEL = 2048
BATCH = 1
SEQ = 8192
DEPTH = 4

GRID_W = 64
CTX_LEN = 256
D_MIX = D_MODEL
NORM_EPS = 1e-6

D_A = D_MIX // 4
A_HEADS = 4
A_HEAD_DIM = D_A // A_HEADS
A_CHUNK = 128

D_B = D_MIX // 4
B_HEADS = 4
B_KEY_DIM = D_B // B_HEADS
B_VAL_DIM = D_B // B_HEADS
B_CHUNK = 64

D_C = D_MIX // 2
C_HEADS = 4
C_HEAD_DIM = D_C // C_HEADS // 2
C_VAL_DIM = 2 * C_HEAD_DIM
Q_BLOCK = 128
ROPE_THETA = 10000.0
ROPE_AXIS_DIM = C_HEAD_DIM // 2

D_IN = 3 * D_A + 5 * D_B + 4 * D_C

kernel_name = 'hybrid_gmlp_hgrn2_diffattn_dit_trunk'


def rms_norm(x, w):
    xf = x.astype(jnp.float32)
    y = xf * lax.rsqrt(jnp.mean(xf * xf, axis=-1, keepdims=True) + NORM_EPS)
    return (y * w.astype(jnp.float32)).astype(x.dtype)


def layer_norm(x, w, b):
    xf = x.astype(jnp.float32)
    mu = jnp.mean(xf, axis=-1, keepdims=True)
    xc = xf - mu
    var = jnp.mean(xc * xc, axis=-1, keepdims=True)
    return (xc * lax.rsqrt(var + NORM_EPS) * w.astype(jnp.float32) + b.astype(jnp.float32)).astype(x.dtype)


def split_columns(p):
    sizes = (D_A, D_A, D_A, D_B, D_B, D_B, D_B, D_B, D_C, D_C, D_C, D_C)
    idx = np.cumsum(sizes)[:-1].tolist()
    return jnp.split(p, idx, axis=-1)


def axial_rope_tables(seq):
    rows = seq // GRID_W
    row_ids = jnp.repeat(jnp.arange(rows, dtype=jnp.float32), GRID_W)
    col_ids = jnp.tile(jnp.arange(GRID_W, dtype=jnp.float32), rows)
    inv = ROPE_THETA ** (-jnp.arange(0, ROPE_AXIS_DIM, 2, dtype=jnp.float32) / ROPE_AXIS_DIM)
    ang_r = row_ids[:, None] * inv[None, :]
    ang_c = col_ids[:, None] * inv[None, :]
    return (jnp.cos(ang_r), jnp.sin(ang_r), jnp.cos(ang_c), jnp.sin(ang_c))


def rope_rotate(x, cos, sin):
    x1, x2 = jnp.split(x.astype(jnp.float32), 2, axis=-1)
    cos = cos[:, None, None, :]
    sin = sin[:, None, None, :]
    return jnp.concatenate([x1 * cos - x2 * sin, x2 * cos + x1 * sin], axis=-1)


def rope_2d(x, tabs):
    cos_r, sin_r, cos_c, sin_c = tabs
    x_row, x_col = jnp.split(x, 2, axis=-1)
    y = jnp.concatenate([rope_rotate(x_row, cos_r, sin_r), rope_rotate(x_col, cos_c, sin_c)], axis=-1)
    return y.astype(x.dtype)


def gmlp_branch(u, v, z, ln_w, ln_b, ws, bs):
    bsz, t, _ = v.shape
    u = jax.nn.gelu(u, approximate=False)
    v = layer_norm(jax.nn.gelu(v, approximate=False), ln_w, ln_b)
    vr = v.reshape(bsz, t // A_CHUNK, A_CHUNK, A_HEADS, A_HEAD_DIM)
    s = jnp.einsum('hts,bnshd->bnthd', ws, vr) + bs.T[None, None, :, :, None]
    s = s.reshape(bsz, t, D_A)
    return u * s * jax.nn.silu(z)


def hgrn_heads(a):
    bsz, t, _ = a.shape
    return a.reshape(bsz, t, B_HEADS, a.shape[-1] // B_HEADS)


def hgrn_gates(a, lb):
    af = a.astype(jnp.float32)
    logf = jnp.logaddexp(jnp.log(lb), jnp.log1p(-lb) + jax.nn.log_sigmoid(af))
    k = (1.0 - lb) * jax.nn.sigmoid(-af)
    return hgrn_heads(k), hgrn_heads(logf)


def hgrn2_chunk_scan(q, k, v, logf, state0):
    bsz, t, h, _ = q.shape
    dv = v.shape[-1]
    n = t // B_CHUNK

    def to_chunks(a):
        return a.astype(jnp.float32).reshape(bsz, n, B_CHUNK, h, a.shape[-1]).transpose(1, 0, 2, 3, 4)

    lower = jnp.tril(jnp.ones((B_CHUNK, B_CHUNK), dtype=bool))[None, :, :, None, None]

    def step(state, inp):
        qc, kc, vc, gc = inp
        b = jnp.cumsum(gc, axis=1)
        o_inter = jnp.einsum('bthk,bhkv->bthv', qc * jnp.exp(b), state)
        decay = jnp.exp(jnp.where(lower, b[:, :, None] - b[:, None, :], -jnp.inf))
        scores = jnp.einsum('bthk,bshk,btshk->bths', qc, kc, decay)
        o_intra = jnp.einsum('bths,bshv->bthv', scores, vc)
        b_last = b[:, -1]
        k_dec = kc * jnp.exp(b_last[:, None] - b)
        new_state = jnp.exp(b_last)[..., None] * state + jnp.einsum('bshk,bshv->bhkv', k_dec, vc)
        return new_state, o_inter + o_intra

    final, out = lax.scan(step, state0, (to_chunks(q), to_chunks(k), to_chunks(v), to_chunks(logf)))
    out = out.transpose(1, 0, 2, 3, 4).reshape(bsz, t, h, dv)
    return final, out


def hgrn_output(o, z, norm_w):
    bsz, t = o.shape[0], o.shape[1]
    o = rms_norm(o.astype(z.dtype), norm_w).reshape(bsz, t, D_B)
    return o * jax.nn.silu(z)


def diff_attend(q, k, v, lam):
    s = jnp.einsum('bqhmd,bkhmd->bhmqk', q * (C_HEAD_DIM ** -0.5), k).astype(jnp.float32)
    p = jax.nn.softmax(s, axis=-1)
    w = p[:, :, 0] - lam * p[:, :, 1]
    return jnp.einsum('bhqk,bkhe->bqhe', w.astype(v.dtype), v)


def diff_output(o, z, subln_w, lam_init):
    bsz, t = o.shape[0], o.shape[1]
    o = (rms_norm(o, subln_w) * (1.0 - lam_init)).reshape(bsz, t, D_C)
    return o * jax.nn.silu(z)


def hybrid_layer(x, xc, c, c_ctx, ada_w, ada_b, norm_w, w_in, g_ln_w, g_ln_b, g_ws, g_bs,
                 lb_fwd, lb_bwd, h_norm_w, lam_params, lam_init, subln_w, w_out, rope, ctx_out):
    bsz, t, _ = x.shape
    shift, scale, gate = jnp.split(jax.nn.silu(c) @ ada_w + ada_b, 3, axis=-1)
    shift_c, scale_c, gate_c = jnp.split(jax.nn.silu(c_ctx) @ ada_w + ada_b, 3, axis=-1)
    h = rms_norm(x, norm_w) * (1.0 + scale[:, None]) + shift[:, None]
    hc = rms_norm(xc, norm_w) * (1.0 + scale_c) + shift_c

    (a_u, a_v, a_z, b_q, b_i, b_ff, b_fb, b_z, c_q, c_k, c_v, c_z) = split_columns(h @ w_in)
    (ac_u, ac_v, ac_z, bc_q, bc_i, bc_ff, bc_fb, bc_z, cc_q, cc_k, cc_v, cc_z) = split_columns(hc @ w_in)

    y_a = gmlp_branch(a_u, a_v, a_z, g_ln_w, g_ln_b, g_ws, g_bs)

    qx, vx = hgrn_heads(jax.nn.silu(b_q)), hgrn_heads(b_i)
    kfx, gfx = hgrn_gates(b_ff, lb_fwd)
    kbx, gbx = hgrn_gates(b_fb, lb_bwd)
    qcx, vcx = hgrn_heads(jax.nn.silu(bc_q)), hgrn_heads(bc_i)
    kfc, gfc = hgrn_gates(bc_ff, lb_fwd)
    kbc, gbc = hgrn_gates(bc_fb, lb_bwd)
    flip = lambda a: jnp.flip(a, axis=1)
    s0 = jnp.zeros((bsz, B_HEADS, B_KEY_DIM, B_VAL_DIM), jnp.float32)
    s_fwd, oc_f = hgrn2_chunk_scan(qcx, kfc, vcx, gfc, s0)
    _, ox_f = hgrn2_chunk_scan(qx, kfx, vx, gfx, s_fwd)
    s_bwd, oc_b = hgrn2_chunk_scan(flip(qcx), flip(kbc), flip(vcx), flip(gbc), s0)
    _, ox_b = hgrn2_chunk_scan(flip(qx), flip(kbx), flip(vx), flip(gbx), s_bwd)
    y_b = hgrn_output(ox_f + flip(ox_b), b_z, h_norm_w)

    lp = lam_params.astype(jnp.float32)
    lam = jnp.exp(jnp.sum(lp[0] * lp[1])) - jnp.exp(jnp.sum(lp[2] * lp[3])) + lam_init
    n_ctx = xc.shape[1]
    q_lat = rope_2d(c_q.reshape(bsz, t, C_HEADS, 2, C_HEAD_DIM), rope)
    k_lat = rope_2d(c_k.reshape(bsz, t, C_HEADS, 2, C_HEAD_DIM), rope)
    v_lat = c_v.reshape(bsz, t, C_HEADS, C_VAL_DIM)
    q_ctx = cc_q.reshape(bsz, n_ctx, C_HEADS, 2, C_HEAD_DIM)
    k_ctx = cc_k.reshape(bsz, n_ctx, C_HEADS, 2, C_HEAD_DIM)
    v_ctx = cc_v.reshape(bsz, n_ctx, C_HEADS, C_VAL_DIM)
    keys = jnp.concatenate([k_lat, k_ctx], axis=1)
    vals = jnp.concatenate([v_lat, v_ctx], axis=1)
    nb = t // Q_BLOCK
    q_blocks = q_lat.reshape(bsz, nb, Q_BLOCK, C_HEADS, 2, C_HEAD_DIM).transpose(1, 0, 2, 3, 4, 5)
    o_blocks = lax.map(lambda qb: diff_attend(qb, keys, vals, lam), q_blocks)
    o_lat = o_blocks.transpose(1, 0, 2, 3, 4).reshape(bsz, t, C_HEADS, C_VAL_DIM)
    y_c = diff_output(o_lat, c_z, subln_w, lam_init)

    y = jnp.concatenate([y_a, y_b, y_c], axis=-1) @ w_out
    x_new = x + gate[:, None] * y

    if ctx_out:
        yc_a = gmlp_branch(ac_u, ac_v, ac_z, g_ln_w, g_ln_b, g_ws, g_bs)
        yc_b = hgrn_output(oc_f + flip(oc_b), bc_z, h_norm_w)
        yc_c = diff_output(diff_attend(q_ctx, k_ctx, v_ctx, lam), cc_z, subln_w, lam_init)
        yc = jnp.concatenate([yc_a, yc_b, yc_c], axis=-1) @ w_out
        xc = xc + gate_c * yc
    return x_new, xc


def setup_inputs(seed: int = 0) -> dict:
    key = jax.random.key(seed)
    ks = jax.random.split(key, 20)
    f32 = jnp.float32

    def nrm(k, shape, s):
        return jax.random.normal(k, shape, f32) * s

    return {
        'x': nrm(ks[0], (BATCH, SEQ, D_MODEL), 1.0),
        'c': nrm(ks[1], (BATCH, D_MODEL), 1.0),
        'ctx': nrm(ks[2], (BATCH, CTX_LEN, D_MODEL), 1.0),
        'c_ctx': nrm(ks[3], (D_MODEL,), 1.0),
        'ada_w': nrm(ks[4], (DEPTH, D_MODEL, 3 * D_MODEL), 0.5 * D_MODEL ** -0.5),
        'ada_b': nrm(ks[5], (DEPTH, 3 * D_MODEL), 0.02),
        'norm_w': 1.0 + nrm(ks[6], (DEPTH, D_MODEL), 0.02),
        'w_in': nrm(ks[7], (DEPTH, D_MODEL, D_IN), D_MODEL ** -0.5),
        'gmlp_ln_w': 1.0 + nrm(ks[8], (DEPTH, D_A), 0.02),
        'gmlp_ln_b': nrm(ks[9], (DEPTH, D_A), 0.02),
        'gmlp_ws': nrm(ks[10], (DEPTH, A_HEADS, A_CHUNK, A_CHUNK), A_CHUNK ** -0.5),
        'gmlp_bs': 1.0 + nrm(ks[11], (DEPTH, A_HEADS, A_CHUNK), 0.02),
        'hgrn_lower_bounds': nrm(ks[12], (2, DEPTH, D_B), 0.1),
        'hgrn_norm_w': 1.0 + nrm(ks[13], (DEPTH, B_VAL_DIM), 0.02),
        'diff_lambda': nrm(ks[14], (DEPTH, 4, C_HEAD_DIM), 0.1),
        'diff_subln_w': 1.0 + nrm(ks[15], (DEPTH, C_VAL_DIM), 0.02),
        'w_out': nrm(ks[16], (DEPTH, D_MIX, D_MODEL), D_MIX ** -0.5),
        'final_norm_w': 1.0 + nrm(ks[17], (D_MODEL,), 0.02),
    }


def reference(x, c, ctx, c_ctx, ada_w, ada_b, norm_w, w_in, gmlp_ln_w, gmlp_ln_b, gmlp_ws, gmlp_bs,
              hgrn_lower_bounds, hgrn_norm_w, diff_lambda, diff_subln_w, w_out, final_norm_w):
    rope = axial_rope_tables(x.shape[1])
    lb_soft = jax.nn.softmax(hgrn_lower_bounds.astype(jnp.float32), axis=1)
    lb_cum = jnp.cumsum(lb_soft, axis=1)
    lb_all = lb_cum - lb_cum[:, :1]
    xc = ctx
    for layer in range(DEPTH):
        lam_init = 0.8 - 0.6 * math.exp(-0.3 * layer)
        x, xc = hybrid_layer(x, xc, c, c_ctx, ada_w[layer], ada_b[layer], norm_w[layer], w_in[layer],
                             gmlp_ln_w[layer], gmlp_ln_b[layer], gmlp_ws[layer], gmlp_bs[layer],
                             lb_all[0, layer], lb_all[1, layer], hgrn_norm_w[layer], diff_lambda[layer],
                             lam_init, diff_subln_w[layer], w_out[layer], rope, layer < DEPTH - 1)
    return rms_norm(x, final_norm_w)
```

```python
import functools
import math

import numpy as np
import jax
import jax.numpy as jnp
from jax import lax
from jax.experimental import pallas as pl
from jax.experimental.pallas import tpu as pltpu

F32 = jnp.float32
BF16 = jnp.bfloat16

D_MODEL = 2048
GRID_W = 64
NORM_EPS = 1e-6
D_A = 512
A_HEADS = 4
A_CHUNK = 128
D_B = 512
B_HEADS = 4
B_DIM = 128
D_C = 1024
C_HEADS = 4
C_HEAD_DIM = 128
C_VAL_DIM = 256
ROPE_THETA = 10000.0
ROPE_AXIS_DIM = 64
D_IN = 3 * D_A + 5 * D_B + 4 * D_C
OFF_AU, OFF_AV, OFF_AZ = 0, 512, 1024
OFF_BQ, OFF_BI, OFF_BFF, OFF_BFB, OFF_BZ = 1536, 2048, 2560, 3072, 3584
OFF_CQ, OFF_CK, OFF_CV, OFF_CZ = 4096, 5120, 6144, 7168
Q_SCALE = C_HEAD_DIM ** -0.5

LANES = 128
HGRN_CHUNK = 64
HGRN_LEVELS = 6
VMEM_LIMIT = 56 * 1024 * 1024


def _cparams(n_axes):
    return pltpu.CompilerParams(
        dimension_semantics=("arbitrary",) * n_axes,
        vmem_limit_bytes=VMEM_LIMIT,
    )


def _silu(v):
    return v * jax.nn.sigmoid(v)


def _nt_dot(a, b):
    return lax.dot_general(a, b, (((1,), (1,)), ((), ())), preferred_element_type=F32)


def _adaln_kernel(c_ref, w_ref, b_ref, o_ref):
    s = _silu(c_ref[...])
    o_ref[...] = (
        jnp.dot(s, w_ref[...], precision=lax.Precision.HIGHEST, preferred_element_type=F32)
        + b_ref[...]
    )


def _adaln(cc, ada_w, ada_b):
    depth, d, d3 = ada_w.shape
    tn = 1024
    return pl.pallas_call(
        _adaln_kernel,
        grid=(depth, d3 // tn),
        in_specs=[
            pl.BlockSpec((8, d), lambda l, j: (0, 0)),
            pl.BlockSpec((None, d, tn), lambda l, j: (l, 0, j)),
            pl.BlockSpec((None, 1, tn), lambda l, j: (l, 0, j)),
        ],
        out_specs=pl.BlockSpec((None, 8, tn), lambda l, j: (l, 0, j)),
        out_shape=jax.ShapeDtypeStruct((depth, 8, d3), F32),
        compiler_params=_cparams(2),
    )(cc, ada_w, ada_b.reshape(depth, 1, d3))


def _lower_bound_kernel(x_ref, o_ref):
    depth = x_ref.shape[1]
    for d in range(2):
        x = x_ref[d]
        e = jnp.exp(x - jnp.max(x, axis=0, keepdims=True))
        soft = e / jnp.sum(e, axis=0, keepdims=True)
        run = jnp.zeros((1, x.shape[1]), F32)
        o_ref[d, 0:1, :] = run
        for l in range(1, depth):
            run = run + soft[l:l + 1, :]
            o_ref[d, l:l + 1, :] = run


def _lower_bounds(hgrn_lower_bounds):
    return pl.pallas_call(
        _lower_bound_kernel,
        out_shape=jax.ShapeDtypeStruct(hgrn_lower_bounds.shape, F32),
    )(hgrn_lower_bounds.astype(F32))


def _inproj_kernel(x_ref, mod_ref, nw_ref, cos_ref, sa_ref, sb_ref, w_ref, o_ref, h_ref,
                   *, n_lat, tm, tn, sub):
    i = pl.program_id(0)
    j = pl.program_id(1)
    d = x_ref.shape[1]

    @pl.when(j == 0)
    def _():
        def body(r, carry):
            rows = pl.ds(pl.multiple_of(r * sub, sub), sub)
            x = x_ref[rows, :]
            ms = jnp.mean(x * x, axis=-1, keepdims=True)
            y = x * lax.rsqrt(ms + NORM_EPS) * nw_ref[...]
            row = i * tm + r * sub + lax.broadcasted_iota(jnp.int32, (sub, 1), 0)
            is_ctx = row >= n_lat
            shift = jnp.where(is_ctx, mod_ref[1:2, 0:d], mod_ref[0:1, 0:d])
            scale = jnp.where(is_ctx, mod_ref[1:2, d:2 * d], mod_ref[0:1, d:2 * d])
            h_ref[rows, :] = (y * (1.0 + scale) + shift).astype(BF16)
            return carry

        lax.fori_loop(0, tm // sub, body, 0)

    col0 = j * tn
    is_q = jnp.logical_and(col0 >= OFF_CQ, col0 < OFF_CQ + D_C)
    is_k = jnp.logical_and(col0 >= OFF_CK, col0 < OFF_CK + D_C)
    is_rope = jnp.logical_or(is_q, is_k)

    @pl.when(jnp.logical_not(is_rope))
    def _():
        o_ref[...] = jnp.dot(h_ref[...], w_ref[...], preferred_element_type=F32).astype(o_ref.dtype)

    @pl.when(is_rope)
    def _():
        acc = jnp.dot(h_ref[...], w_ref[...], preferred_element_type=F32)
        mult = jnp.where(is_q, Q_SCALE, 1.0).astype(F32)
        c = cos_ref[...] * mult
        sa = sa_ref[...] * mult
        sb = sb_ref[...] * mult
        for g in range(tn // LANES):
            xg = acc[:, g * LANES:(g + 1) * LANES]
            yg = xg * c + pltpu.roll(xg, 32, 1) * sa + pltpu.roll(xg, LANES - 32, 1) * sb
            o_ref[:, g * LANES:(g + 1) * LANES] = yg.astype(o_ref.dtype)


def _inproj(xa, mod, norm_w, cos_t, sa_t, sb_t, w_in_bf, *, n_lat, tm, tn):
    rows, d = xa.shape
    d_in = w_in_bf.shape[1]
    kern = functools.partial(_inproj_kernel, n_lat=n_lat, tm=tm, tn=tn, sub=128)
    return pl.pallas_call(
        kern,
        grid=(rows // tm, d_in // tn),
        in_specs=[
            pl.BlockSpec((tm, d), lambda i, j: (i, 0)),
            pl.BlockSpec(mod.shape, lambda i, j: (0, 0)),
            pl.BlockSpec((1, d), lambda i, j: (0, 0)),
            pl.BlockSpec((tm, LANES), lambda i, j: (i, 0)),
            pl.BlockSpec((tm, LANES), lambda i, j: (i, 0)),
            pl.BlockSpec((tm, LANES), lambda i, j: (i, 0)),
            pl.BlockSpec((d, tn), lambda i, j: (0, j)),
        ],
        out_specs=pl.BlockSpec((tm, tn), lambda i, j: (i, j)),
        out_shape=jax.ShapeDtypeStruct((rows, d_in), BF16),
        scratch_shapes=[pltpu.VMEM((tm, d), BF16)],
        compiler_params=_cparams(2),
    )(xa, mod, norm_w.reshape(1, d), cos_t, sa_t, sb_t, w_in_bf)


def _attn_kernel(lam_ref, sw_ref, q_ref, k_ref, v_ref, z_ref, *rest, tk, nk, lam_init):
    o_ref, m_ref, l_ref, acc_ref = rest[-4:]
    m_ref[...] = jnp.full(m_ref.shape, -jnp.inf, F32)
    l_ref[...] = jnp.zeros(l_ref.shape, F32)
    acc_ref[...] = jnp.zeros(acc_ref.shape, F32)

    def body(c, carry):
        rows = pl.ds(pl.multiple_of(c * tk, tk), tk)
        v = v_ref[rows, :]
        for u in range(2):
            q = q_ref[:, u * C_HEAD_DIM:(u + 1) * C_HEAD_DIM]
            k = k_ref[rows, u * C_HEAD_DIM:(u + 1) * C_HEAD_DIM]
            s = _nt_dot(q, k)
            m_prev = m_ref[u]
            m_new = jnp.maximum(m_prev, jnp.max(s, axis=-1, keepdims=True))
            alpha = jnp.exp(m_prev - m_new)
            p = jnp.exp(s - m_new)
            l_ref[u] = alpha * l_ref[u] + jnp.sum(p, axis=-1, keepdims=True)
            acc_ref[u] = alpha * acc_ref[u] + jnp.dot(p.astype(BF16), v, preferred_element_type=F32)
            m_ref[u] = m_new
        return carry

    lax.fori_loop(0, nk, body, 0)

    lp = lam_ref[...]
    lam = (jnp.exp(jnp.sum(lp[0:1] * lp[1:2], axis=-1, keepdims=True))
           - jnp.exp(jnp.sum(lp[2:3] * lp[3:4], axis=-1, keepdims=True)) + lam_init)
    o = acc_ref[0] / l_ref[0] - lam * (acc_ref[1] / l_ref[1])
    ms = jnp.mean(o * o, axis=-1, keepdims=True)
    y = o * lax.rsqrt(ms + NORM_EPS) * sw_ref[...] * (1.0 - lam_init)
    o_ref[...] = (y * _silu(z_ref[...].astype(F32))).astype(o_ref.dtype)


def _attention(p, lam_p, subln_w, prev, *, q_row0, n_q, kv_row0, n_kv, tq, tk, lam_init):
    rows = p.shape[0]
    qb0, kb0 = q_row0 // tq, kv_row0 // n_kv
    cq, ck, cv, cz = (OFF_CQ // C_VAL_DIM, OFF_CK // C_VAL_DIM, OFF_CV // C_VAL_DIM, OFF_CZ // C_VAL_DIM)
    kern = functools.partial(_attn_kernel, tk=tk, nk=n_kv // tk, lam_init=lam_init)
    in_specs = [
        pl.BlockSpec((4, C_HEAD_DIM), lambda h, i: (0, 0)),
        pl.BlockSpec((1, C_VAL_DIM), lambda h, i: (0, 0)),
        pl.BlockSpec((tq, C_VAL_DIM), lambda h, i: (qb0 + i, cq + h)),
        pl.BlockSpec((n_kv, C_VAL_DIM), lambda h, i: (kb0, ck + h)),
        pl.BlockSpec((n_kv, C_VAL_DIM), lambda h, i: (kb0, cv + h)),
        pl.BlockSpec((tq, C_VAL_DIM), lambda h, i: (qb0 + i, cz + h)),
    ]
    args = [lam_p, subln_w.reshape(1, C_VAL_DIM), p, p, p, p]
    aliases = {}
    if prev is not None:
        in_specs.append(pl.BlockSpec(memory_space=pl.ANY))
        args.append(prev)
        aliases = {6: 0}
    return pl.pallas_call(
        kern,
        grid=(C_HEADS, n_q // tq),
        in_specs=in_specs,
        out_specs=pl.BlockSpec((tq, C_VAL_DIM), lambda h, i: (qb0 + i, h)),
        out_shape=jax.ShapeDtypeStruct((rows, D_C), BF16),
        scratch_shapes=[
            pltpu.VMEM((2, tq, 1), F32),
            pltpu.VMEM((2, tq, 1), F32),
            pltpu.VMEM((2, tq, C_VAL_DIM), F32),
        ],
        input_output_aliases=aliases,
        compiler_params=_cparams(2),
    )(*args)


def _hgrn_constants():
    c = HGRN_CHUNK
    t = np.arange(c)
    mats = [(t[None, :] <= t[:, None]), (t[None, :] > t[:, None])]
    masks, rts = [], []
    for lvl in range(HGRN_LEVELS):
        n = (c // 2) >> lvl
        later = (t & n) != 0
        parent = t // (2 * n)
        mid = parent * 2 * n + n - 1
        dm = np.zeros((c, c), bool)
        for tt in range(c):
            if later[tt]:
                dm[tt, mid[tt] + 1:tt + 1] = True
            else:
                dm[tt, tt + 1:mid[tt] + 1] = True
        mats.append(dm)
        masks.append(later[:, None] & (~later[None, :]) & (parent[:, None] == parent[None, :]))
        rts.append(later)
    masks.append(np.eye(c, dtype=bool))
    cm = np.concatenate(mats, axis=0).astype(np.float32)
    mk = np.stack(masks).astype(np.float32)
    rt = np.repeat(np.stack(rts).astype(np.float32)[:, :, None], LANES, axis=2)

    def flip(a):
        blocks = a.reshape(-1, c, a.shape[-1])
        if a.shape[-1] == c:
            blocks = blocks[:, ::-1, ::-1]
        else:
            blocks = blocks[:, ::-1, :]
        return blocks.reshape(a.shape)

    cm2 = np.stack([cm, flip(cm)])
    mk2 = np.stack([mk, flip(mk.reshape(-1, c)).reshape(mk.shape)])
    rt2 = np.stack([rt, flip(rt.reshape(-1, LANES)).reshape(rt.shape)])
    return cm2, mk2, rt2


def _hgrn_direction(d, rows, a_ref, q_ref, v_ref, lb_ref, cm_ref, mk_ref, rt_ref, st_ref, o_ref):
    c = HGRN_CHUNK
    a = a_ref[rows, :].astype(F32)
    lb = lb_ref[...]
    log_lb = jnp.log(lb)
    log_1m = jnp.log1p(-lb)
    log_sig = jnp.minimum(a, 0.0) - jnp.log1p(jnp.exp(-jnp.abs(a)))
    q2 = log_1m + log_sig
    g = jnp.maximum(log_lb, q2) + jnp.log1p(jnp.exp(-jnp.abs(log_lb - q2)))
    kk = (1.0 - lb) * jax.nn.sigmoid(-a)
    qs = _silu(q_ref[rows, :].astype(F32))
    v32 = v_ref[rows, :].astype(F32)
    vb = v32.astype(BF16)

    e = jnp.dot(cm_ref[d], g, precision=lax.Precision.HIGHEST, preferred_element_type=F32)
    f = jnp.exp(e)
    qe = (qs * f[0:c]).astype(BF16)
    kdec = (kk * f[c:2 * c]).astype(BF16)
    last = c - 1 if d == 0 else 0
    e_last = f[last:last + 1, :]
    qsb = qs.astype(BF16)
    kkb = kk.astype(BF16)

    for h in range(B_HEADS):
        ln = slice(h * B_DIM, (h + 1) * B_DIM)
        sc = mk_ref[d, HGRN_LEVELS] * _nt_dot(qsb[:, ln], kkb[:, ln])
        for lvl in range(HGRN_LEVELS):
            fl = f[(2 + lvl) * c:(3 + lvl) * c, ln]
            xl = (jnp.where(rt_ref[d, lvl] != 0.0, qs[:, ln], kk[:, ln]) * fl).astype(BF16)
            sc = sc + mk_ref[d, lvl] * _nt_dot(xl, xl)
        st = st_ref[d, h]
        o = (jnp.dot(sc.astype(BF16), vb[:, ln], preferred_element_type=F32)
             + _nt_dot(qe[:, ln], st.astype(BF16)))
        st_ref[d, h] = st * e_last[:, ln] + jnp.dot(
            v32[:, ln].T.astype(BF16), kdec[:, ln], preferred_element_type=F32)
        o_ref[rows, ln] = o.astype(o_ref.dtype)


def _hgrn_kernel(lbf_ref, lbb_ref, cm_ref, mk_ref, rt_ref,
                 qf_ref, vf_ref, af_ref, qb_ref, vb_ref, ab_ref,
                 of_ref, ob_ref, st_ref, *, n_sub):
    @pl.when(pl.program_id(0) == 0)
    def _():
        st_ref[...] = jnp.zeros(st_ref.shape, F32)

    def body(cidx, carry):
        rf = pl.ds(pl.multiple_of(cidx * HGRN_CHUNK, HGRN_CHUNK), HGRN_CHUNK)
        rb = pl.ds(pl.multiple_of((n_sub - 1 - cidx) * HGRN_CHUNK, HGRN_CHUNK), HGRN_CHUNK)
        _hgrn_direction(0, rf, af_ref, qf_ref, vf_ref, lbf_ref, cm_ref, mk_ref, rt_ref, st_ref, of_ref)
        _hgrn_direction(1, rb, ab_ref, qb_ref, vb_ref, lbb_ref, cm_ref, mk_ref, rt_ref, st_ref, ob_ref)
        return carry

    lax.fori_loop(0, n_sub, body, 0)


def _hgrn(p, lb_f, lb_b, consts, *, n_lat, n_ctx, tr):
    rows = p.shape[0]
    cm, mk, rt = consts
    n_lat_b, n_ctx_b = n_lat // tr, n_ctx // tr
    nb = n_lat_b + n_ctx_b

    def fwd_blk(s):
        return jnp.where(s < n_ctx_b, n_lat_b + s, s - n_ctx_b)

    def bwd_blk(s):
        return nb - 1 - s

    cq, ci, cff, cfb = OFF_BQ // D_B, OFF_BI // D_B, OFF_BFF // D_B, OFF_BFB // D_B
    full = lambda a: pl.BlockSpec(a.shape, lambda s: (0,) * a.ndim)
    kern = functools.partial(_hgrn_kernel, n_sub=tr // HGRN_CHUNK)
    return pl.pallas_call(
        kern,
        grid=(nb,),
        in_specs=[
            pl.BlockSpec((1, D_B), lambda s: (0, 0)),
            pl.BlockSpec((1, D_B), lambda s: (0, 0)),
            full(cm), full(mk), full(rt),
            pl.BlockSpec((tr, D_B), lambda s: (fwd_blk(s), cq)),
            pl.BlockSpec((tr, D_B), lambda s: (fwd_blk(s), ci)),
            pl.BlockSpec((tr, D_B), lambda s: (fwd_blk(s), cff)),
            pl.BlockSpec((tr, D_B), lambda s: (bwd_blk(s), cq)),
            pl.BlockSpec((tr, D_B), lambda s: (bwd_blk(s), ci)),
            pl.BlockSpec((tr, D_B), lambda s: (bwd_blk(s), cfb)),
        ],
        out_specs=[
            pl.BlockSpec((tr, D_B), lambda s: (fwd_blk(s), 0)),
            pl.BlockSpec((tr, D_B), lambda s: (bwd_blk(s), 0)),
        ],
        out_shape=[jax.ShapeDtypeStruct((rows, D_B), F32)] * 2,
        scratch_shapes=[pltpu.VMEM((2, B_HEADS, B_DIM, B_DIM), F32)],
        compiler_params=_cparams(1),
    )(lb_f, lb_b, cm, mk, rt, p, p, p, p, p, p)


def _gelu(v):
    return 0.5 * v * (1.0 + lax.erf(v * (2.0 ** -0.5)))


def _out_kernel(x_ref, mod_ref, u_ref, v_ref, z_ref, bz_ref, of_ref, ob_ref, yc_ref,
                lnw_ref, lnb_ref, ws_ref, bs_ref, hw_ref, wout_ref, fw_ref,
                o_ref, y_ref, *, n_lat, tm, final):
    i = pl.program_id(0)
    d = x_ref.shape[1]

    def body(n, carry):
        rows = pl.ds(pl.multiple_of(n * A_CHUNK, A_CHUNK), A_CHUNK)
        u = _gelu(u_ref[rows, :].astype(F32))
        v = _gelu(v_ref[rows, :].astype(F32))
        mu = jnp.mean(v, axis=-1, keepdims=True)
        vc = v - mu
        var = jnp.mean(vc * vc, axis=-1, keepdims=True)
        vln = (vc * lax.rsqrt(var + NORM_EPS) * lnw_ref[...] + lnb_ref[...]).astype(BF16)
        gz = _silu(z_ref[rows, :].astype(F32))
        ob = of_ref[rows, :] + ob_ref[rows, :]
        gbz = _silu(bz_ref[rows, :].astype(F32))
        for h in range(A_HEADS):
            ln = slice(h * LANES, (h + 1) * LANES)
            s = jnp.dot(ws_ref[h], vln[:, ln], preferred_element_type=F32) + bs_ref[h]
            y_ref[rows, h * LANES:(h + 1) * LANES] = (u[:, ln] * s * gz[:, ln]).astype(BF16)
            oh = ob[:, ln]
            ms = jnp.mean(oh * oh, axis=-1, keepdims=True)
            yb = oh * lax.rsqrt(ms + NORM_EPS) * hw_ref[...] * gbz[:, ln]
            y_ref[rows, D_A + h * LANES:D_A + (h + 1) * LANES] = yb.astype(BF16)
        return carry

    lax.fori_loop(0, tm // A_CHUNK, body, 0)
    y_ref[:, D_A + D_B:] = yc_ref[...]

    out = jnp.dot(y_ref[...], wout_ref[...], preferred_element_type=F32)
    row = i * tm + lax.broadcasted_iota(jnp.int32, (tm, 1), 0)
    gate = jnp.where(row >= n_lat, mod_ref[1:2, 2 * d:3 * d], mod_ref[0:1, 2 * d:3 * d])
    xn = x_ref[...] + gate * out
    if final:
        ms = jnp.mean(xn * xn, axis=-1, keepdims=True)
        xn = xn * lax.rsqrt(ms + NORM_EPS) * fw_ref[...]
    o_ref[...] = xn


def _outproj(xa, mod, p, o_f, o_b, yc, ln_w, ln_b, ws_bf, bs_b, hnorm_w, w_out_bf, final_w,
             *, n_lat, n_rows, tm, final):
    d = xa.shape[1]
    kern = functools.partial(_out_kernel, n_lat=n_lat, tm=tm, final=final)
    cu, cv, cz, cbz = OFF_AU // D_A, OFF_AV // D_A, OFF_AZ // D_A, OFF_BZ // D_B
    full = lambda a: pl.BlockSpec(a.shape, lambda i: (0,) * a.ndim)
    return pl.pallas_call(
        kern,
        grid=(n_rows // tm,),
        in_specs=[
            pl.BlockSpec((tm, d), lambda i: (i, 0)),
            full(mod),
            pl.BlockSpec((tm, D_A), lambda i: (i, cu)),
            pl.BlockSpec((tm, D_A), lambda i: (i, cv)),
            pl.BlockSpec((tm, D_A), lambda i: (i, cz)),
            pl.BlockSpec((tm, D_B), lambda i: (i, cbz)),
            pl.BlockSpec((tm, D_B), lambda i: (i, 0)),
            pl.BlockSpec((tm, D_B), lambda i: (i, 0)),
            pl.BlockSpec((tm, D_C), lambda i: (i, 0)),
            full(ln_w), full(ln_b), full(ws_bf), full(bs_b), full(hnorm_w), full(w_out_bf), full(final_w),
        ],
        out_specs=pl.BlockSpec((tm, d), lambda i: (i, 0)),
        out_shape=jax.ShapeDtypeStruct((n_rows, d), F32),
        scratch_shapes=[pltpu.VMEM((tm, d), BF16)],
        compiler_params=_cparams(1),
    )(xa, mod, p, p, p, p, o_f, o_b, yc, ln_w, ln_b, ws_bf, bs_b, hnorm_w, w_out_bf, final_w)


def _rope_tables(n_lat, n_ctx):
    rows = n_lat // GRID_W
    row_ids = jnp.repeat(jnp.arange(rows, dtype=F32), GRID_W)
    col_ids = jnp.tile(jnp.arange(GRID_W, dtype=F32), rows)
    inv = ROPE_THETA ** (-jnp.arange(0, ROPE_AXIS_DIM, 2, dtype=F32) / ROPE_AXIS_DIM)
    ang_r = row_ids[:, None] * inv[None, :]
    ang_c = col_ids[:, None] * inv[None, :]
    zero = jnp.zeros_like(ang_r)
    cos_t = jnp.concatenate([jnp.cos(ang_r), jnp.cos(ang_r), jnp.cos(ang_c), jnp.cos(ang_c)], axis=1)
    sa_t = jnp.concatenate([zero, jnp.sin(ang_r), zero, jnp.sin(ang_c)], axis=1)
    sb_t = jnp.concatenate([-jnp.sin(ang_r), zero, -jnp.sin(ang_c), zero], axis=1)
    pad = lambda a, v: jnp.concatenate([a, jnp.full((n_ctx, LANES), v, F32)], axis=0)
    return pad(cos_t, 1.0), pad(sa_t, 0.0), pad(sb_t, 0.0)


def _pick_tile(n, candidates):
    for t in candidates:
        if n % t == 0:
            return t
    raise ValueError(f"no tile for {n}")


def kernel(x, c, ctx, c_ctx, ada_w, ada_b, norm_w, w_in, gmlp_ln_w, gmlp_ln_b, gmlp_ws, gmlp_bs,
           hgrn_lower_bounds, hgrn_norm_w, diff_lambda, diff_subln_w, w_out, final_norm_w):
    bsz, n_lat, d = x.shape
    n_ctx = ctx.shape[1]
    depth = ada_w.shape[0]
    assert bsz == 1 and d == D_MODEL
    assert n_lat % 512 == 0 and n_ctx % 256 == 0 and n_lat % GRID_W == 0
    n_rows = n_lat + n_ctx

    xa = jnp.concatenate([x[0], ctx[0]], axis=0)
    cc = jnp.zeros((8, d), F32).at[0].set(c[0]).at[1].set(c_ctx)
    mods = _adaln(cc, ada_w, ada_b)
    lb_all = _lower_bounds(hgrn_lower_bounds)
    cos_t, sa_t, sb_t = _rope_tables(n_lat, n_ctx)
    consts = tuple(jnp.asarray(a) for a in _hgrn_constants())

    w_in_bf = w_in.astype(BF16)
    w_out_bf = w_out.astype(BF16)
    ws_bf = gmlp_ws.astype(BF16)
    bs_b = jnp.broadcast_to(gmlp_bs[:, :, :, None], gmlp_bs.shape + (LANES,))

    tm_in = _pick_tile(n_rows, (768, 256))
    tm_out = _pick_tile(n_rows, (256,))
    tm_fin = _pick_tile(n_lat, (512,))
    tq = _pick_tile(n_lat, (512,))
    tk = _pick_tile(n_rows, (768, 256))

    for layer in range(depth):
        lam_init = 0.8 - 0.6 * math.exp(-0.3 * layer)
        last = layer == depth - 1
        p = _inproj(xa, mods[layer], norm_w[layer], cos_t, sa_t, sb_t, w_in_bf[layer],
                    n_lat=n_lat, tm=tm_in, tn=1024)
        yc = _attention(p, diff_lambda[layer], diff_subln_w[layer], None,
                        q_row0=0, n_q=n_lat, kv_row0=0, n_kv=n_rows, tq=tq, tk=tk, lam_init=lam_init)
        if not last:
            yc = _attention(p, diff_lambda[layer], diff_subln_w[layer], yc,
                            q_row0=n_lat, n_q=n_ctx, kv_row0=n_lat, n_kv=n_ctx,
                            tq=n_ctx, tk=n_ctx, lam_init=lam_init)
        o_f, o_b = _hgrn(p, lb_all[0, layer].reshape(1, D_B), lb_all[1, layer].reshape(1, D_B), consts,
                         n_lat=n_lat, n_ctx=n_ctx, tr=256)
        xa = _outproj(xa, mods[layer], p, o_f, o_b, yc,
                      gmlp_ln_w[layer].reshape(1, D_A), gmlp_ln_b[layer].reshape(1, D_A),
                      ws_bf[layer], bs_b[layer], hgrn_norm_w[layer].reshape(1, B_DIM),
                      w_out_bf[layer], final_norm_w.reshape(1, d),
                      n_lat=n_lat, n_rows=n_lat if last else n_rows,
                      tm=tm_fin if last else tm_out, final=last)
    return xa[None]
```

```python
import functools
import math

import numpy as np
import jax
import jax.numpy as jnp
from jax import lax
from jax.experimental import pallas as pl
from jax.experimental.pallas import tpu as pltpu

F32 = jnp.float32
BF16 = jnp.bfloat16

D_MODEL = 2048
GRID_W = 64
NORM_EPS = 1e-6
D_A = 512
A_HEADS = 4
A_CHUNK = 128
D_B = 512
B_HEADS = 4
B_DIM = 128
D_C = 1024
C_HEADS = 4
C_HEAD_DIM = 128
C_VAL_DIM = 256
ROPE_THETA = 10000.0
ROPE_AXIS_DIM = 64
D_IN = 3 * D_A + 5 * D_B + 4 * D_C
OFF_AU, OFF_AV, OFF_AZ = 0, 512, 1024
OFF_BQ, OFF_BI, OFF_BFF, OFF_BFB, OFF_BZ = 1536, 2048, 2560, 3072, 3584
OFF_CQ, OFF_CK, OFF_CV, OFF_CZ = 4096, 5120, 6144, 7168
Q_SCALE = C_HEAD_DIM ** -0.5 * math.log2(math.e)

LANES = 128
HGRN_CHUNK = 64
HGRN_LEVELS = 6
ATTN_ROW_BLOCK = 32
VMEM_LIMIT = 56 * 1024 * 1024


def _cparams(n_axes):
    return pltpu.CompilerParams(
        dimension_semantics=("arbitrary",) * n_axes,
        vmem_limit_bytes=VMEM_LIMIT,
    )


def _silu(v):
    return v * jax.nn.sigmoid(v)


def _nt_dot(a, b):
    return lax.dot_general(a, b, (((1,), (1,)), ((), ())), preferred_element_type=F32)


def _adaln_kernel(c_ref, w_ref, b_ref, o_ref):
    s = _silu(c_ref[...])
    o_ref[...] = (
        jnp.dot(s, w_ref[...], precision=lax.Precision.HIGHEST, preferred_element_type=F32)
        + b_ref[...]
    )


def _adaln(cc, ada_w, ada_b):
    depth, d, d3 = ada_w.shape
    tn = 1024
    return pl.pallas_call(
        _adaln_kernel,
        grid=(depth, d3 // tn),
        in_specs=[
            pl.BlockSpec((8, d), lambda l, j: (0, 0)),
            pl.BlockSpec((None, d, tn), lambda l, j: (l, 0, j)),
            pl.BlockSpec((None, 1, tn), lambda l, j: (l, 0, j)),
        ],
        out_specs=pl.BlockSpec((None, 8, tn), lambda l, j: (l, 0, j)),
        out_shape=jax.ShapeDtypeStruct((depth, 8, d3), F32),
        compiler_params=_cparams(2),
    )(cc, ada_w, ada_b.reshape(depth, 1, d3))


def _lower_bound_kernel(x_ref, o_ref):
    depth = x_ref.shape[1]
    for d in range(2):
        x = x_ref[d]
        e = jnp.exp(x - jnp.max(x, axis=0, keepdims=True))
        soft = e / jnp.sum(e, axis=0, keepdims=True)
        run = jnp.zeros((1, x.shape[1]), F32)
        o_ref[d, 0:1, :] = run
        for l in range(1, depth):
            run = run + soft[l:l + 1, :]
            o_ref[d, l:l + 1, :] = run


def _lower_bounds(hgrn_lower_bounds):
    return pl.pallas_call(
        _lower_bound_kernel,
        out_shape=jax.ShapeDtypeStruct(hgrn_lower_bounds.shape, F32),
    )(hgrn_lower_bounds.astype(F32))


def _inproj_kernel(x_ref, mod_ref, nw_ref, cos_ref, sa_ref, sb_ref, w_ref, o_ref, h_ref,
                   *, n_lat, tm, tn, sub):
    i = pl.program_id(0)
    j = pl.program_id(1)
    d = x_ref.shape[1]

    @pl.when(j == 0)
    def _():
        def body(r, carry):
            rows = pl.ds(pl.multiple_of(r * sub, sub), sub)
            x = x_ref[rows, :]
            ms = jnp.mean(x * x, axis=-1, keepdims=True)
            y = x * lax.rsqrt(ms + NORM_EPS) * nw_ref[...]
            row = i * tm + r * sub + lax.broadcasted_iota(jnp.int32, (sub, 1), 0)
            is_ctx = row >= n_lat
            shift = jnp.where(is_ctx, mod_ref[1:2, 0:d], mod_ref[0:1, 0:d])
            scale = jnp.where(is_ctx, mod_ref[1:2, d:2 * d], mod_ref[0:1, d:2 * d])
            h_ref[rows, :] = (y * (1.0 + scale) + shift).astype(BF16)
            return carry

        lax.fori_loop(0, tm // sub, body, 0)

    col0 = j * tn
    is_q = jnp.logical_and(col0 >= OFF_CQ, col0 < OFF_CQ + D_C)
    is_k = jnp.logical_and(col0 >= OFF_CK, col0 < OFF_CK + D_C)
    is_rope = jnp.logical_or(is_q, is_k)

    @pl.when(jnp.logical_not(is_rope))
    def _():
        o_ref[...] = jnp.dot(h_ref[...], w_ref[...], preferred_element_type=F32).astype(o_ref.dtype)

    @pl.when(is_rope)
    def _():
        acc = jnp.dot(h_ref[...], w_ref[...], preferred_element_type=F32)
        mult = jnp.where(is_q, Q_SCALE, 1.0).astype(F32)
        c = cos_ref[...] * mult
        sa = sa_ref[...] * mult
        sb = sb_ref[...] * mult
        for g in range(tn // LANES):
            xg = acc[:, g * LANES:(g + 1) * LANES]
            yg = xg * c + pltpu.roll(xg, 32, 1) * sa + pltpu.roll(xg, LANES - 32, 1) * sb
            o_ref[:, g * LANES:(g + 1) * LANES] = yg.astype(o_ref.dtype)


def _inproj(xa, mod, norm_w, cos_t, sa_t, sb_t, w_in_bf, *, n_lat, tm, tn):
    rows, d = xa.shape
    d_in = w_in_bf.shape[1]
    kern = functools.partial(_inproj_kernel, n_lat=n_lat, tm=tm, tn=tn, sub=128)
    return pl.pallas_call(
        kern,
        grid=(rows // tm, d_in // tn),
        in_specs=[
            pl.BlockSpec((tm, d), lambda i, j: (i, 0)),
            pl.BlockSpec(mod.shape, lambda i, j: (0, 0)),
            pl.BlockSpec((1, d), lambda i, j: (0, 0)),
            pl.BlockSpec((tm, LANES), lambda i, j: (i, 0)),
            pl.BlockSpec((tm, LANES), lambda i, j: (i, 0)),
            pl.BlockSpec((tm, LANES), lambda i, j: (i, 0)),
            pl.BlockSpec((d, tn), lambda i, j: (0, j)),
        ],
        out_specs=pl.BlockSpec((tm, tn), lambda i, j: (i, j)),
        out_shape=jax.ShapeDtypeStruct((rows, d_in), BF16),
        scratch_shapes=[pltpu.VMEM((tm, d), BF16)],
        compiler_params=_cparams(2),
    )(xa, mod, norm_w.reshape(1, d), cos_t, sa_t, sb_t, w_in_bf)


def _attn_kernel(lam_ref, sw_ref, q_ref, k_ref, v_ref, z_ref, *rest, tk, n_main, tail, lam_init, rb):
    o_ref, m_ref, l_ref, acc_ref, sa_ref, sb_ref, mpa_ref, mpb_ref, p_ref, al_ref = rest[-10:]
    tq = q_ref.shape[0]
    m_ref[...] = jnp.full(m_ref.shape, -jnp.inf, F32)
    l_ref[...] = jnp.zeros(l_ref.shape, F32)
    acc_ref[...] = jnp.zeros(acc_ref.shape, F32)
    cw = 2 * LANES

    def scores(row0, size, s_ref, mp_ref):
        for u in range(2):
            q = q_ref[:, u * C_HEAD_DIM:(u + 1) * C_HEAD_DIM]
            mp = None
            for cb in range(size // cw):
                k = k_ref[pl.ds(row0 + cb * cw, cw), u * C_HEAD_DIM:(u + 1) * C_HEAD_DIM]
                sc = _nt_dot(q, k)
                s_ref[u, :, cb * cw:(cb + 1) * cw] = sc
                part = jnp.maximum(sc[:, 0:LANES], sc[:, LANES:cw])
                mp = part if mp is None else jnp.maximum(mp, part)
            mp_ref[u] = mp

    def softmax_pv(row0, size, s_ref, mp_ref):
        v = v_ref[pl.ds(row0, size), :]
        for u in range(2):
            for r0 in range(0, tq, rb):
                r = slice(r0, r0 + rb)
                m_prev = m_ref[u, r, :]
                m_new = jnp.maximum(m_prev, jnp.max(mp_ref[u, r, :], axis=-1, keepdims=True))
                alpha = jnp.exp2(m_prev - m_new)
                p = jnp.exp2(s_ref[u, r, 0:size] - m_new)
                lp = p[:, 0:LANES]
                for g in range(1, size // LANES):
                    lp = lp + p[:, g * LANES:(g + 1) * LANES]
                l_ref[u, r, :] = alpha * l_ref[u, r, :] + lp
                p_ref[u, r, 0:size] = p.astype(BF16)
                al_ref[u, r, :] = alpha
                m_ref[u, r, :] = m_new
            acc_ref[u] = al_ref[u] * acc_ref[u] + jnp.dot(p_ref[u, :, 0:size], v, preferred_element_type=F32)

    def main_row(c):
        return pl.multiple_of(c * tk, tk)

    if n_main:
        scores(0, tk, sa_ref, mpa_ref)

        def body(c2, carry):
            c = 2 * c2
            scores(main_row(c + 1), tk, sb_ref, mpb_ref)
            softmax_pv(main_row(c), tk, sa_ref, mpa_ref)
            scores(main_row(c + 2), tk, sa_ref, mpa_ref)
            softmax_pv(main_row(c + 1), tk, sb_ref, mpb_ref)
            return carry

        lax.fori_loop(0, n_main // 2 - 1, body, 0)
        c = n_main - 2
        scores(c * tk + tk, tk, sb_ref, mpb_ref)
        softmax_pv(c * tk, tk, sa_ref, mpa_ref)
        if tail:
            scores(n_main * tk, tail, sa_ref, mpa_ref)
        softmax_pv(c * tk + tk, tk, sb_ref, mpb_ref)
        if tail:
            softmax_pv(n_main * tk, tail, sa_ref, mpa_ref)
    else:
        scores(0, tail, sa_ref, mpa_ref)
        softmax_pv(0, tail, sa_ref, mpa_ref)

    lp = lam_ref[...]
    lam = (jnp.exp(jnp.sum(lp[0:1] * lp[1:2], axis=-1, keepdims=True))
           - jnp.exp(jnp.sum(lp[2:3] * lp[3:4], axis=-1, keepdims=True)) + lam_init)
    l0 = jnp.sum(l_ref[0], axis=-1, keepdims=True)
    l1 = jnp.sum(l_ref[1], axis=-1, keepdims=True)
    o = acc_ref[0] / l0 - lam * (acc_ref[1] / l1)
    ms = jnp.mean(o * o, axis=-1, keepdims=True)
    y = o * lax.rsqrt(ms + NORM_EPS) * sw_ref[...] * (1.0 - lam_init)
    o_ref[...] = (y * _silu(z_ref[...].astype(F32))).astype(o_ref.dtype)


def _attention(p, lam_p, subln_w, prev, *, q_row0, n_q, kv_row0, n_kv, tq, tk, lam_init):
    rows = p.shape[0]
    qb0, kb0 = q_row0 // tq, kv_row0 // n_kv
    n_main = (n_kv // tk) // 2 * 2
    tail = n_kv - n_main * tk
    assert tail % (2 * LANES) == 0 and tail <= max(tk, n_kv if not n_main else 0)
    ts = max(tk if n_main else 0, tail)
    cq, ck, cv, cz = (OFF_CQ // C_VAL_DIM, OFF_CK // C_VAL_DIM, OFF_CV // C_VAL_DIM, OFF_CZ // C_VAL_DIM)
    kern = functools.partial(_attn_kernel, tk=tk, n_main=n_main, tail=tail, lam_init=lam_init,
                             rb=ATTN_ROW_BLOCK)
    in_specs = [
        pl.BlockSpec((4, C_HEAD_DIM), lambda h, i: (0, 0)),
        pl.BlockSpec((1, C_VAL_DIM), lambda h, i: (0, 0)),
        pl.BlockSpec((tq, C_VAL_DIM), lambda h, i: (qb0 + i, cq + h)),
        pl.BlockSpec((n_kv, C_VAL_DIM), lambda h, i: (kb0, ck + h)),
        pl.BlockSpec((n_kv, C_VAL_DIM), lambda h, i: (kb0, cv + h)),
        pl.BlockSpec((tq, C_VAL_DIM), lambda h, i: (qb0 + i, cz + h)),
    ]
    args = [lam_p, subln_w.reshape(1, C_VAL_DIM), p, p, p, p]
    aliases = {}
    if prev is not None:
        in_specs.append(pl.BlockSpec(memory_space=pl.ANY))
        args.append(prev)
        aliases = {6: 0}
    return pl.pallas_call(
        kern,
        grid=(C_HEADS, n_q // tq),
        in_specs=in_specs,
        out_specs=pl.BlockSpec((tq, C_VAL_DIM), lambda h, i: (qb0 + i, h)),
        out_shape=jax.ShapeDtypeStruct((rows, D_C), BF16),
        scratch_shapes=[
            pltpu.VMEM((2, tq, 1), F32),
            pltpu.VMEM((2, tq, LANES), F32),
            pltpu.VMEM((2, tq, C_VAL_DIM), F32),
            pltpu.VMEM((2, tq, ts), F32),
            pltpu.VMEM((2, tq, ts), F32),
            pltpu.VMEM((2, tq, LANES), F32),
            pltpu.VMEM((2, tq, LANES), F32),
            pltpu.VMEM((2, tq, ts), BF16),
            pltpu.VMEM((2, tq, 1), F32),
        ],
        input_output_aliases=aliases,
        compiler_params=_cparams(2),
    )(*args)


def _hgrn_constants():
    c = HGRN_CHUNK
    t = np.arange(c)
    mats = [(t[None, :] <= t[:, None]), (t[None, :] > t[:, None])]
    masks, rts = [], []
    for lvl in range(HGRN_LEVELS):
        n = (c // 2) >> lvl
        later = (t & n) != 0
        parent = t // (2 * n)
        mid = parent * 2 * n + n - 1
        dm = np.zeros((c, c), bool)
        for tt in range(c):
            if later[tt]:
                dm[tt, mid[tt] + 1:tt + 1] = True
            else:
                dm[tt, tt + 1:mid[tt] + 1] = True
        mats.append(dm)
        masks.append(later[:, None] & (~later[None, :]) & (parent[:, None] == parent[None, :]))
        rts.append(later)
    masks.append(np.eye(c, dtype=bool))
    cm = np.concatenate(mats, axis=0).astype(np.float32)
    mk = np.stack(masks).astype(np.float32)
    rt = np.repeat(np.stack(rts).astype(np.float32)[:, :, None], LANES, axis=2)

    def flip(a):
        blocks = a.reshape(-1, c, a.shape[-1])
        if a.shape[-1] == c:
            blocks = blocks[:, ::-1, ::-1]
        else:
            blocks = blocks[:, ::-1, :]
        return blocks.reshape(a.shape)

    cm2 = np.stack([cm, flip(cm)])
    mk2 = np.stack([mk, flip(mk.reshape(-1, c)).reshape(mk.shape)])
    rt2 = np.stack([rt, flip(rt.reshape(-1, LANES)).reshape(rt.shape)])
    return cm2, mk2, rt2


def _hgrn_direction(d, rows, a_ref, q_ref, v_ref, lb_ref, cm_ref, mk_ref, rt_ref, st_ref, o_ref):
    c = HGRN_CHUNK
    a = a_ref[rows, :].astype(F32)
    lb = lb_ref[...]
    log_lb = jnp.log(lb)
    log_1m = jnp.log1p(-lb)
    log_sig = jnp.minimum(a, 0.0) - jnp.log1p(jnp.exp(-jnp.abs(a)))
    q2 = log_1m + log_sig
    g = jnp.maximum(log_lb, q2) + jnp.log1p(jnp.exp(-jnp.abs(log_lb - q2)))
    kk = (1.0 - lb) * jax.nn.sigmoid(-a)
    qs = _silu(q_ref[rows, :].astype(F32))
    v32 = v_ref[rows, :].astype(F32)
    vb = v32.astype(BF16)

    e = jnp.dot(cm_ref[d], g, precision=lax.Precision.HIGHEST, preferred_element_type=F32)
    f = jnp.exp(e)
    qe = (qs * f[0:c]).astype(BF16)
    kdec = (kk * f[c:2 * c]).astype(BF16)
    last = c - 1 if d == 0 else 0
    e_last = f[last:last + 1, :]
    qsb = qs.astype(BF16)
    kkb = kk.astype(BF16)

    for h in range(B_HEADS):
        ln = slice(h * B_DIM, (h + 1) * B_DIM)
        sc = mk_ref[d, HGRN_LEVELS] * _nt_dot(qsb[:, ln], kkb[:, ln])
        for lvl in range(HGRN_LEVELS):
            fl = f[(2 + lvl) * c:(3 + lvl) * c, ln]
            xl = (jnp.where(rt_ref[d, lvl] != 0.0, qs[:, ln], kk[:, ln]) * fl).astype(BF16)
            sc = sc + mk_ref[d, lvl] * _nt_dot(xl, xl)
        st = st_ref[d, h]
        o = (jnp.dot(sc.astype(BF16), vb[:, ln], preferred_element_type=F32)
             + _nt_dot(qe[:, ln], st.astype(BF16)))
        st_ref[d, h] = st * e_last[:, ln] + jnp.dot(
            v32[:, ln].T.astype(BF16), kdec[:, ln], preferred_element_type=F32)
        o_ref[rows, ln] = o.astype(o_ref.dtype)


def _hgrn_kernel(lbf_ref, lbb_ref, cm_ref, mk_ref, rt_ref,
                 qf_ref, vf_ref, af_ref, qb_ref, vb_ref, ab_ref,
                 of_ref, ob_ref, st_ref, *, n_sub):
    @pl.when(pl.program_id(0) == 0)
    def _():
        st_ref[...] = jnp.zeros(st_ref.shape, F32)

    def body(cidx, carry):
        rf = pl.ds(pl.multiple_of(cidx * HGRN_CHUNK, HGRN_CHUNK), HGRN_CHUNK)
        rb = pl.ds(pl.multiple_of((n_sub - 1 - cidx) * HGRN_CHUNK, HGRN_CHUNK), HGRN_CHUNK)
        _hgrn_direction(0, rf, af_ref, qf_ref, vf_ref, lbf_ref, cm_ref, mk_ref, rt_ref, st_ref, of_ref)
        _hgrn_direction(1, rb, ab_ref, qb_ref, vb_ref, lbb_ref, cm_ref, mk_ref, rt_ref, st_ref, ob_ref)
        return carry

    lax.fori_loop(0, n_sub, body, 0)


def _hgrn(p, lb_f, lb_b, consts, *, n_lat, n_ctx, tr):
    rows = p.shape[0]
    cm, mk, rt = consts
    n_lat_b, n_ctx_b = n_lat // tr, n_ctx // tr
    nb = n_lat_b + n_ctx_b

    def fwd_blk(s):
        return jnp.where(s < n_ctx_b, n_lat_b + s, s - n_ctx_b)

    def bwd_blk(s):
        return nb - 1 - s

    cq, ci, cff, cfb = OFF_BQ // D_B, OFF_BI // D_B, OFF_BFF // D_B, OFF_BFB // D_B
    full = lambda a: pl.BlockSpec(a.shape, lambda s: (0,) * a.ndim)
    kern = functools.partial(_hgrn_kernel, n_sub=tr // HGRN_CHUNK)
    return pl.pallas_call(
        kern,
        grid=(nb,),
        in_specs=[
            pl.BlockSpec((1, D_B), lambda s: (0, 0)),
            pl.BlockSpec((1, D_B), lambda s: (0, 0)),
            full(cm), full(mk), full(rt),
            pl.BlockSpec((tr, D_B), lambda s: (fwd_blk(s), cq)),
            pl.BlockSpec((tr, D_B), lambda s: (fwd_blk(s), ci)),
            pl.BlockSpec((tr, D_B), lambda s: (fwd_blk(s), cff)),
            pl.BlockSpec((tr, D_B), lambda s: (bwd_blk(s), cq)),
            pl.BlockSpec((tr, D_B), lambda s: (bwd_blk(s), ci)),
            pl.BlockSpec((tr, D_B), lambda s: (bwd_blk(s), cfb)),
        ],
        out_specs=[
            pl.BlockSpec((tr, D_B), lambda s: (fwd_blk(s), 0)),
            pl.BlockSpec((tr, D_B), lambda s: (bwd_blk(s), 0)),
        ],
        out_shape=[jax.ShapeDtypeStruct((rows, D_B), F32)] * 2,
        scratch_shapes=[pltpu.VMEM((2, B_HEADS, B_DIM, B_DIM), F32)],
        compiler_params=_cparams(1),
    )(lb_f, lb_b, cm, mk, rt, p, p, p, p, p, p)


def _gelu(v):
    return 0.5 * v * (1.0 + lax.erf(v * (2.0 ** -0.5)))


def _out_kernel(x_ref, mod_ref, u_ref, v_ref, z_ref, bz_ref, of_ref, ob_ref, yc_ref,
                lnw_ref, lnb_ref, ws_ref, bs_ref, hw_ref, wout_ref, fw_ref,
                o_ref, y_ref, *, n_lat, tm, final):
    i = pl.program_id(0)
    d = x_ref.shape[1]

    def body(n, carry):
        rows = pl.ds(pl.multiple_of(n * A_CHUNK, A_CHUNK), A_CHUNK)
        u = _gelu(u_ref[rows, :].astype(F32))
        v = _gelu(v_ref[rows, :].astype(F32))
        mu = jnp.mean(v, axis=-1, keepdims=True)
        vc = v - mu
        var = jnp.mean(vc * vc, axis=-1, keepdims=True)
        vln = (vc * lax.rsqrt(var + NORM_EPS) * lnw_ref[...] + lnb_ref[...]).astype(BF16)
        gz = _silu(z_ref[rows, :].astype(F32))
        ob = of_ref[rows, :] + ob_ref[rows, :]
        gbz = _silu(bz_ref[rows, :].astype(F32))
        for h in range(A_HEADS):
            ln = slice(h * LANES, (h + 1) * LANES)
            s = jnp.dot(ws_ref[h], vln[:, ln], preferred_element_type=F32) + bs_ref[h]
            y_ref[rows, h * LANES:(h + 1) * LANES] = (u[:, ln] * s * gz[:, ln]).astype(BF16)
            oh = ob[:, ln]
            ms = jnp.mean(oh * oh, axis=-1, keepdims=True)
            yb = oh * lax.rsqrt(ms + NORM_EPS) * hw_ref[...] * gbz[:, ln]
            y_ref[rows, D_A + h * LANES:D_A + (h + 1) * LANES] = yb.astype(BF16)
        return carry

    lax.fori_loop(0, tm // A_CHUNK, body, 0)
    y_ref[:, D_A + D_B:] = yc_ref[...]

    out = jnp.dot(y_ref[...], wout_ref[...], preferred_element_type=F32)
    row = i * tm + lax.broadcasted_iota(jnp.int32, (tm, 1), 0)
    gate = jnp.where(row >= n_lat, mod_ref[1:2, 2 * d:3 * d], mod_ref[0:1, 2 * d:3 * d])
    xn = x_ref[...] + gate * out
    if final:
        ms = jnp.mean(xn * xn, axis=-1, keepdims=True)
        xn = xn * lax.rsqrt(ms + NORM_EPS) * fw_ref[...]
    o_ref[...] = xn


def _outproj(xa, mod, p, o_f, o_b, yc, ln_w, ln_b, ws_bf, bs_b, hnorm_w, w_out_bf, final_w,
             *, n_lat, n_rows, tm, final):
    d = xa.shape[1]
    kern = functools.partial(_out_kernel, n_lat=n_lat, tm=tm, final=final)
    cu, cv, cz, cbz = OFF_AU // D_A, OFF_AV // D_A, OFF_AZ // D_A, OFF_BZ // D_B
    full = lambda a: pl.BlockSpec(a.shape, lambda i: (0,) * a.ndim)
    return pl.pallas_call(
        kern,
        grid=(n_rows // tm,),
        in_specs=[
            pl.BlockSpec((tm, d), lambda i: (i, 0)),
            full(mod),
            pl.BlockSpec((tm, D_A), lambda i: (i, cu)),
            pl.BlockSpec((tm, D_A), lambda i: (i, cv)),
            pl.BlockSpec((tm, D_A), lambda i: (i, cz)),
            pl.BlockSpec((tm, D_B), lambda i: (i, cbz)),
            pl.BlockSpec((tm, D_B), lambda i: (i, 0)),
            pl.BlockSpec((tm, D_B), lambda i: (i, 0)),
            pl.BlockSpec((tm, D_C), lambda i: (i, 0)),
            full(ln_w), full(ln_b), full(ws_bf), full(bs_b), full(hnorm_w), full(w_out_bf), full(final_w),
        ],
        out_specs=pl.BlockSpec((tm, d), lambda i: (i, 0)),
        out_shape=jax.ShapeDtypeStruct((n_rows, d), F32),
        scratch_shapes=[pltpu.VMEM((tm, d), BF16)],
        compiler_params=_cparams(1),
    )(xa, mod, p, p, p, p, o_f, o_b, yc, ln_w, ln_b, ws_bf, bs_b, hnorm_w, w_out_bf, final_w)


def _rope_tables(n_lat, n_ctx):
    rows = n_lat // GRID_W
    row_ids = jnp.repeat(jnp.arange(rows, dtype=F32), GRID_W)
    col_ids = jnp.tile(jnp.arange(GRID_W, dtype=F32), rows)
    inv = ROPE_THETA ** (-jnp.arange(0, ROPE_AXIS_DIM, 2, dtype=F32) / ROPE_AXIS_DIM)
    ang_r = row_ids[:, None] * inv[None, :]
    ang_c = col_ids[:, None] * inv[None, :]
    zero = jnp.zeros_like(ang_r)
    cos_t = jnp.concatenate([jnp.cos(ang_r), jnp.cos(ang_r), jnp.cos(ang_c), jnp.cos(ang_c)], axis=1)
    sa_t = jnp.concatenate([zero, jnp.sin(ang_r), zero, jnp.sin(ang_c)], axis=1)
    sb_t = jnp.concatenate([-jnp.sin(ang_r), zero, -jnp.sin(ang_c), zero], axis=1)
    pad = lambda a, v: jnp.concatenate([a, jnp.full((n_ctx, LANES), v, F32)], axis=0)
    return pad(cos_t, 1.0), pad(sa_t, 0.0), pad(sb_t, 0.0)


def _pick_tile(n, candidates):
    for t in candidates:
        if n % t == 0:
            return t
    raise ValueError(f"no tile for {n}")


def kernel(x, c, ctx, c_ctx, ada_w, ada_b, norm_w, w_in, gmlp_ln_w, gmlp_ln_b, gmlp_ws, gmlp_bs,
           hgrn_lower_bounds, hgrn_norm_w, diff_lambda, diff_subln_w, w_out, final_norm_w):
    bsz, n_lat, d = x.shape
    n_ctx = ctx.shape[1]
    depth = ada_w.shape[0]
    assert bsz == 1 and d == D_MODEL
    assert n_lat % 512 == 0 and n_ctx % 256 == 0 and n_lat % GRID_W == 0
    n_rows = n_lat + n_ctx

    xa = jnp.concatenate([x[0], ctx[0]], axis=0)
    cc = jnp.zeros((8, d), F32).at[0].set(c[0]).at[1].set(c_ctx)
    mods = _adaln(cc, ada_w, ada_b)
    lb_all = _lower_bounds(hgrn_lower_bounds)
    cos_t, sa_t, sb_t = _rope_tables(n_lat, n_ctx)
    consts = tuple(jnp.asarray(a) for a in _hgrn_constants())

    w_in_bf = w_in.astype(BF16)
    w_out_bf = w_out.astype(BF16)
    ws_bf = gmlp_ws.astype(BF16)
    bs_b = jnp.broadcast_to(gmlp_bs[:, :, :, None], gmlp_bs.shape + (LANES,))

    tm_in = _pick_tile(n_rows, (768, 256))
    tm_out = _pick_tile(n_rows, (256,))
    tm_fin = _pick_tile(n_lat, (512,))
    tq = _pick_tile(n_lat, (512,))
    tk = 2048 if n_lat >= 8192 else 512

    for layer in range(depth):
        lam_init = 0.8 - 0.6 * math.exp(-0.3 * layer)
        last = layer == depth - 1
        p = _inproj(xa, mods[layer], norm_w[layer], cos_t, sa_t, sb_t, w_in_bf[layer],
                    n_lat=n_lat, tm=tm_in, tn=1024)
        yc = _attention(p, diff_lambda[layer], diff_subln_w[layer], None,
                        q_row0=0, n_q=n_lat, kv_row0=0, n_kv=n_rows, tq=tq, tk=tk, lam_init=lam_init)
        if not last:
            yc = _attention(p, diff_lambda[layer], diff_subln_w[layer], yc,
                            q_row0=n_lat, n_q=n_ctx, kv_row0=n_lat, n_kv=n_ctx,
                            tq=n_ctx, tk=n_ctx, lam_init=lam_init)
        o_f, o_b = _hgrn(p, lb_all[0, layer].reshape(1, D_B), lb_all[1, layer].reshape(1, D_B), consts,
                         n_lat=n_lat, n_ctx=n_ctx, tr=256)
        xa = _outproj(xa, mods[layer], p, o_f, o_b, yc,
                      gmlp_ln_w[layer].reshape(1, D_A), gmlp_ln_b[layer].reshape(1, D_A),
                      ws_bf[layer], bs_b[layer], hgrn_norm_w[layer].reshape(1, B_DIM),
                      w_out_bf[layer], final_norm_w.reshape(1, d),
                      n_lat=n_lat, n_rows=n_lat if last else n_rows,
                      tm=tm_fin if last else tm_out, final=last)
    return xa[None]
```

```python
import functools
import math

import numpy as np
import jax
import jax.numpy as jnp
from jax import lax
from jax.experimental import pallas as pl
from jax.experimental.pallas import tpu as pltpu

F32 = jnp.float32
BF16 = jnp.bfloat16

D_MODEL = 2048
GRID_W = 64
NORM_EPS = 1e-6
D_A = 512
A_HEADS = 4
A_CHUNK = 128
D_B = 512
B_HEADS = 4
B_DIM = 128
D_C = 1024
C_HEADS = 4
C_HEAD_DIM = 128
C_VAL_DIM = 256
ROPE_THETA = 10000.0
ROPE_AXIS_DIM = 64
D_IN = 3 * D_A + 5 * D_B + 4 * D_C
OFF_AU, OFF_AV, OFF_AZ = 0, 512, 1024
OFF_BQ, OFF_BI, OFF_BFF, OFF_BFB, OFF_BZ = 1536, 2048, 2560, 3072, 3584
OFF_CQ, OFF_CK, OFF_CV, OFF_CZ = 4096, 5120, 6144, 7168
Q_SCALE = C_HEAD_DIM ** -0.5 * math.log2(math.e)

LANES = 128
HGRN_CHUNK = 64
HGRN_LEVELS = 6
LOG2_E = math.log2(math.e)
ATTN_ROW_BLOCK = 32
VMEM_LIMIT = 56 * 1024 * 1024


def _cparams(n_axes):
    return pltpu.CompilerParams(
        dimension_semantics=("arbitrary",) * n_axes,
        vmem_limit_bytes=VMEM_LIMIT,
    )


def _silu(v):
    return v * jax.nn.sigmoid(v)


def _nt_dot(a, b):
    return lax.dot_general(a, b, (((1,), (1,)), ((), ())), preferred_element_type=F32)


def _adaln_kernel(c_ref, w_ref, b_ref, o_ref):
    s = _silu(c_ref[...])
    o_ref[...] = (
        jnp.dot(s, w_ref[...], precision=lax.Precision.HIGHEST, preferred_element_type=F32)
        + b_ref[...]
    )


def _adaln(cc, ada_w, ada_b):
    depth, d, d3 = ada_w.shape
    tn = 1024
    return pl.pallas_call(
        _adaln_kernel,
        grid=(depth, d3 // tn),
        in_specs=[
            pl.BlockSpec((8, d), lambda l, j: (0, 0)),
            pl.BlockSpec((None, d, tn), lambda l, j: (l, 0, j)),
            pl.BlockSpec((None, 1, tn), lambda l, j: (l, 0, j)),
        ],
        out_specs=pl.BlockSpec((None, 8, tn), lambda l, j: (l, 0, j)),
        out_shape=jax.ShapeDtypeStruct((depth, 8, d3), F32),
        compiler_params=_cparams(2),
    )(cc, ada_w, ada_b.reshape(depth, 1, d3))


def _lower_bound_kernel(x_ref, o_ref):
    depth = x_ref.shape[1]
    for d in range(2):
        x = x_ref[d]
        e = jnp.exp(x - jnp.max(x, axis=0, keepdims=True))
        soft = e / jnp.sum(e, axis=0, keepdims=True)
        run = jnp.zeros((1, x.shape[1]), F32)
        o_ref[d, 0:1, :] = run
        for l in range(1, depth):
            run = run + soft[l:l + 1, :]
            o_ref[d, l:l + 1, :] = run


def _lower_bounds(hgrn_lower_bounds):
    return pl.pallas_call(
        _lower_bound_kernel,
        out_shape=jax.ShapeDtypeStruct(hgrn_lower_bounds.shape, F32),
    )(hgrn_lower_bounds.astype(F32))


def _inproj_kernel(x_ref, mod_ref, nw_ref, cos_ref, sa_ref, sb_ref, w_ref, o_ref, h_ref,
                   *, n_lat, tm, tn, sub):
    i = pl.program_id(0)
    j = pl.program_id(1)
    d = x_ref.shape[1]

    @pl.when(j == 0)
    def _():
        def body(r, carry):
            rows = pl.ds(pl.multiple_of(r * sub, sub), sub)
            x = x_ref[rows, :]
            ms = jnp.mean(x * x, axis=-1, keepdims=True)
            y = x * lax.rsqrt(ms + NORM_EPS) * nw_ref[...]
            row = i * tm + r * sub + lax.broadcasted_iota(jnp.int32, (sub, 1), 0)
            is_ctx = row >= n_lat
            shift = jnp.where(is_ctx, mod_ref[1:2, 0:d], mod_ref[0:1, 0:d])
            scale = jnp.where(is_ctx, mod_ref[1:2, d:2 * d], mod_ref[0:1, d:2 * d])
            h_ref[rows, :] = (y * (1.0 + scale) + shift).astype(BF16)
            return carry

        lax.fori_loop(0, tm // sub, body, 0)

    col0 = j * tn
    is_q = jnp.logical_and(col0 >= OFF_CQ, col0 < OFF_CQ + D_C)
    is_k = jnp.logical_and(col0 >= OFF_CK, col0 < OFF_CK + D_C)
    is_rope = jnp.logical_or(is_q, is_k)

    @pl.when(jnp.logical_not(is_rope))
    def _():
        o_ref[...] = jnp.dot(h_ref[...], w_ref[...], preferred_element_type=F32).astype(o_ref.dtype)

    @pl.when(is_rope)
    def _():
        acc = jnp.dot(h_ref[...], w_ref[...], preferred_element_type=F32)
        mult = jnp.where(is_q, Q_SCALE, 1.0).astype(F32)
        c = cos_ref[...] * mult
        sa = sa_ref[...] * mult
        sb = sb_ref[...] * mult
        for g in range(tn // LANES):
            xg = acc[:, g * LANES:(g + 1) * LANES]
            yg = xg * c + pltpu.roll(xg, 32, 1) * sa + pltpu.roll(xg, LANES - 32, 1) * sb
            o_ref[:, g * LANES:(g + 1) * LANES] = yg.astype(o_ref.dtype)


def _inproj(xa, mods, norm_w, cos_t, sa_t, sb_t, w_in_bf, *, layer, n_lat, tm, tn):
    rows, d = xa.shape
    d_in = w_in_bf.shape[2]
    kern = functools.partial(_inproj_kernel, n_lat=n_lat, tm=tm, tn=tn, sub=128)
    return pl.pallas_call(
        kern,
        grid=(rows // tm, d_in // tn),
        in_specs=[
            pl.BlockSpec((tm, d), lambda i, j: (i, 0)),
            pl.BlockSpec((None,) + mods.shape[1:], lambda i, j: (layer, 0, 0)),
            pl.BlockSpec((None, 1, d), lambda i, j: (layer, 0, 0)),
            pl.BlockSpec((tm, LANES), lambda i, j: (i, 0)),
            pl.BlockSpec((tm, LANES), lambda i, j: (i, 0)),
            pl.BlockSpec((tm, LANES), lambda i, j: (i, 0)),
            pl.BlockSpec((None, d, tn), lambda i, j: (layer, 0, j)),
        ],
        out_specs=pl.BlockSpec((tm, tn), lambda i, j: (i, j)),
        out_shape=jax.ShapeDtypeStruct((rows, d_in), BF16),
        scratch_shapes=[pltpu.VMEM((tm, d), BF16)],
        compiler_params=_cparams(2),
    )(xa, mods, norm_w, cos_t, sa_t, sb_t, w_in_bf)


def _attn_kernel(lam_ref, sw_ref, q_ref, k_ref, v_ref, z_ref,
                 o_ref, m_ref, l_ref, acc_ref, sa_ref, sb_ref, mpa_ref, mpb_ref, p_ref, al_ref,
                 *, tk, n_main, tail, lam_init, rb):
    tq = q_ref.shape[0]
    m_ref[...] = jnp.full(m_ref.shape, -jnp.inf, F32)
    l_ref[...] = jnp.zeros(l_ref.shape, F32)
    acc_ref[...] = jnp.zeros(acc_ref.shape, F32)
    cw = 2 * LANES

    def scores(row0, size, s_ref, mp_ref):
        for u in range(2):
            q = q_ref[:, u * C_HEAD_DIM:(u + 1) * C_HEAD_DIM]
            mp = None
            for cb in range(size // cw):
                k = k_ref[pl.ds(row0 + cb * cw, cw), u * C_HEAD_DIM:(u + 1) * C_HEAD_DIM]
                sc = _nt_dot(q, k)
                s_ref[u, :, cb * cw:(cb + 1) * cw] = sc
                part = jnp.maximum(sc[:, 0:LANES], sc[:, LANES:cw])
                mp = part if mp is None else jnp.maximum(mp, part)
            mp_ref[u] = mp

    def softmax_pv(row0, size, s_ref, mp_ref):
        v = v_ref[pl.ds(row0, size), :]
        for u in range(2):
            for r0 in range(0, tq, rb):
                r = slice(r0, r0 + rb)
                m_prev = m_ref[u, r, :]
                m_new = jnp.maximum(m_prev, jnp.max(mp_ref[u, r, :], axis=-1, keepdims=True))
                alpha = jnp.exp2(m_prev - m_new)
                p = jnp.exp2(s_ref[u, r, 0:size] - m_new)
                lp = p[:, 0:LANES]
                for g in range(1, size // LANES):
                    lp = lp + p[:, g * LANES:(g + 1) * LANES]
                l_ref[u, r, :] = alpha * l_ref[u, r, :] + lp
                p_ref[u, r, 0:size] = p.astype(BF16)
                al_ref[u, r, :] = alpha
                m_ref[u, r, :] = m_new
            acc_ref[u] = al_ref[u] * acc_ref[u] + jnp.dot(p_ref[u, :, 0:size], v, preferred_element_type=F32)

    def main_row(c):
        return pl.multiple_of(c * tk, tk)

    if n_main:
        scores(0, tk, sa_ref, mpa_ref)

        def body(c2, carry):
            c = 2 * c2
            scores(main_row(c + 1), tk, sb_ref, mpb_ref)
            softmax_pv(main_row(c), tk, sa_ref, mpa_ref)
            scores(main_row(c + 2), tk, sa_ref, mpa_ref)
            softmax_pv(main_row(c + 1), tk, sb_ref, mpb_ref)
            return carry

        lax.fori_loop(0, n_main // 2 - 1, body, 0)
        c = n_main - 2
        scores(c * tk + tk, tk, sb_ref, mpb_ref)
        softmax_pv(c * tk, tk, sa_ref, mpa_ref)
        if tail:
            scores(n_main * tk, tail, sa_ref, mpa_ref)
        softmax_pv(c * tk + tk, tk, sb_ref, mpb_ref)
        if tail:
            softmax_pv(n_main * tk, tail, sa_ref, mpa_ref)
    else:
        scores(0, tail, sa_ref, mpa_ref)
        softmax_pv(0, tail, sa_ref, mpa_ref)

    lp = lam_ref[...]
    lam = (jnp.exp(jnp.sum(lp[0:1] * lp[1:2], axis=-1, keepdims=True))
           - jnp.exp(jnp.sum(lp[2:3] * lp[3:4], axis=-1, keepdims=True)) + lam_init)
    l0 = jnp.sum(l_ref[0], axis=-1, keepdims=True)
    l1 = jnp.sum(l_ref[1], axis=-1, keepdims=True)
    o = acc_ref[0] / l0 - lam * (acc_ref[1] / l1)
    ms = jnp.mean(o * o, axis=-1, keepdims=True)
    y = o * lax.rsqrt(ms + NORM_EPS) * sw_ref[...] * (1.0 - lam_init)
    o_ref[...] = (y * _silu(z_ref[...].astype(F32))).astype(o_ref.dtype)


def _attention(p, lam_p, subln_w, *, layer, q_row0, n_q, kv_row0, n_kv, tq, tk, lam_init):
    qb0, kb0 = q_row0 // tq, kv_row0 // n_kv
    n_main = (n_kv // tk) // 2 * 2
    tail = n_kv - n_main * tk
    assert tail % (2 * LANES) == 0 and tail <= max(tk, n_kv if not n_main else 0)
    ts = max(tk if n_main else 0, tail)
    cq, ck, cv, cz = (OFF_CQ // C_VAL_DIM, OFF_CK // C_VAL_DIM, OFF_CV // C_VAL_DIM, OFF_CZ // C_VAL_DIM)
    kern = functools.partial(_attn_kernel, tk=tk, n_main=n_main, tail=tail, lam_init=lam_init,
                             rb=ATTN_ROW_BLOCK)
    in_specs = [
        pl.BlockSpec((None, 4, C_HEAD_DIM), lambda h, i: (layer, 0, 0)),
        pl.BlockSpec((None, 1, C_VAL_DIM), lambda h, i: (layer, 0, 0)),
        pl.BlockSpec((tq, C_VAL_DIM), lambda h, i: (qb0 + i, cq + h)),
        pl.BlockSpec((n_kv, C_VAL_DIM), lambda h, i: (kb0, ck + h)),
        pl.BlockSpec((n_kv, C_VAL_DIM), lambda h, i: (kb0, cv + h)),
        pl.BlockSpec((tq, C_VAL_DIM), lambda h, i: (qb0 + i, cz + h)),
    ]
    return pl.pallas_call(
        kern,
        grid=(C_HEADS, n_q // tq),
        in_specs=in_specs,
        out_specs=pl.BlockSpec((tq, C_VAL_DIM), lambda h, i: (i, h)),
        out_shape=jax.ShapeDtypeStruct((n_q, D_C), BF16),
        scratch_shapes=[
            pltpu.VMEM((2, tq, 1), F32),
            pltpu.VMEM((2, tq, LANES), F32),
            pltpu.VMEM((2, tq, C_VAL_DIM), F32),
            pltpu.VMEM((2, tq, ts), F32),
            pltpu.VMEM((2, tq, ts), F32),
            pltpu.VMEM((2, tq, LANES), F32),
            pltpu.VMEM((2, tq, LANES), F32),
            pltpu.VMEM((2, tq, ts), BF16),
            pltpu.VMEM((2, tq, 1), F32),
        ],
        compiler_params=_cparams(2),
    )(lam_p, subln_w, p, p, p, p)


def _hgrn_constants():
    c = HGRN_CHUNK
    t = np.arange(c)
    mats = [(t[None, :] <= t[:, None]), (t[None, :] > t[:, None])]
    masks, rts = [], []
    for lvl in range(HGRN_LEVELS):
        n = (c // 2) >> lvl
        later = (t & n) != 0
        parent = t // (2 * n)
        mid = parent * 2 * n + n - 1
        dm = np.zeros((c, c), bool)
        for tt in range(c):
            if later[tt]:
                dm[tt, mid[tt] + 1:tt + 1] = True
            else:
                dm[tt, tt + 1:mid[tt] + 1] = True
        mats.append(dm)
        masks.append(later[:, None] & (~later[None, :]) & (parent[:, None] == parent[None, :]))
        rts.append(later)
    masks.append(np.eye(c, dtype=bool))
    cm = np.concatenate(mats, axis=0).astype(np.float32)
    mk = np.stack(masks).astype(np.float32)
    rt = np.repeat(np.stack(rts).astype(np.float32)[:, :, None], LANES, axis=2)

    def flip(a):
        blocks = a.reshape(-1, c, a.shape[-1])
        if a.shape[-1] == c:
            blocks = blocks[:, ::-1, ::-1]
        else:
            blocks = blocks[:, ::-1, :]
        return blocks.reshape(a.shape)

    cm2 = np.stack([cm, flip(cm)])
    cm2 = np.concatenate([cm2, cm2, cm2], axis=2)
    mk2 =np.stack([mk, flip(mk.reshape(-1, c)).reshape(mk.shape)])
    rt2 = np.stack([rt, flip(rt.reshape(-1, LANES)).reshape(rt.shape)])
    return cm2, mk2, rt2


def _hgrn_kernel(lbf_ref, lbb_ref, cm_ref, mk_ref, rt_ref,
                 qf_ref, vf_ref, af_ref, qb_ref, vb_ref, ab_ref,
                 of_ref, ob_ref, st_ref, f_ref, qs_ref, kk_ref, sc_ref, *, n_sub):
    c = HGRN_CHUNK

    @pl.when(pl.program_id(0) == 0)
    def _():
        st_ref[...] = jnp.zeros(st_ref.shape, F32)

    dirs = ((0, af_ref, qf_ref, vf_ref, lbf_ref, of_ref), (1, ab_ref, qb_ref, vb_ref, lbb_ref, ob_ref))

    def chunk_rows(d, t):
        blk = t if d == 0 else n_sub - 1 - t
        return slice(blk * c, (blk + 1) * c)

    for d, a_ref, q_ref, v_ref, lb_ref, o_ref in dirs:
        a = a_ref[...].astype(F32)
        lb = lb_ref[...]
        log_lb = jnp.log(lb)
        log_1m = jnp.log1p(-lb)
        log_sig = jnp.minimum(a, 0.0) - jnp.log(1.0 + jnp.exp(-jnp.abs(a)))
        q2 = log_1m + log_sig
        g = jnp.maximum(log_lb, q2) + jnp.log(1.0 + jnp.exp(-jnp.abs(log_lb - q2)))
        g = g * LOG2_E
        kk_ref[d] = (1.0 - lb) * jax.nn.sigmoid(-a)
        qs_ref[d] = _silu(q_ref[...].astype(F32))
        g1 = g.astype(BF16)
        r1 = g - g1.astype(F32)
        g2 = r1.astype(BF16)
        g3 = (r1 - g2.astype(F32)).astype(BF16)
        for t in range(n_sub):
            r = chunk_rows(d, t)
            gcat = jnp.concatenate([g1[r], g2[r], g3[r]], axis=0)
            f_ref[d, t] = jnp.exp2(jnp.dot(cm_ref[d], gcat, preferred_element_type=F32))

    for d, a_ref, q_ref, v_ref, lb_ref, o_ref in dirs:
        q_side = [rt_ref[d, lvl] != 0.0 for lvl in range(HGRN_LEVELS)]
        for t in range(n_sub):
            r = chunk_rows(d, t)
            for h in range(B_HEADS):
                ln = slice(h * B_DIM, (h + 1) * B_DIM)
                qs = qs_ref[d, r, ln]
                kk = kk_ref[d, r, ln]
                sc = mk_ref[d, HGRN_LEVELS] * _nt_dot(qs.astype(BF16), kk.astype(BF16))
                for lvl in range(HGRN_LEVELS):
                    fl = f_ref[d, t, (2 + lvl) * c:(3 + lvl) * c, ln]
                    xl = (jnp.where(q_side[lvl], qs, kk) * fl).astype(BF16)
                    sc = sc + mk_ref[d, lvl] * _nt_dot(xl, xl)
                sc_ref[d, t, h] = sc.astype(BF16)

    for t in range(n_sub):
        for d, a_ref, q_ref, v_ref, lb_ref, o_ref in dirs:
            r = chunk_rows(d, t)
            last = c - 1 if d == 0 else 0
            for h in range(B_HEADS):
                ln = slice(h * B_DIM, (h + 1) * B_DIM)
                v32 = v_ref[r, ln].astype(F32)
                qe = (qs_ref[d, r, ln] * f_ref[d, t, 0:c, ln]).astype(BF16)
                kdec = (kk_ref[d, r, ln] * f_ref[d, t, c:2 * c, ln]).astype(BF16)
                e_last = f_ref[d, t, last:last + 1, ln]
                st = st_ref[d, h]
                o = (jnp.dot(sc_ref[d, t, h], v32.astype(BF16), preferred_element_type=F32)
                     + _nt_dot(qe, st.astype(BF16)))
                st_ref[d, h] = st * e_last + jnp.dot(v32.T.astype(BF16), kdec, preferred_element_type=F32)
                o_ref[r, ln] = o.astype(o_ref.dtype)


def _hgrn(p, lb_all, consts, *, layer, n_lat, n_ctx, tr):
    rows = p.shape[0]
    cm, mk, rt = consts
    n_lat_b, n_ctx_b = n_lat // tr, n_ctx // tr
    nb = n_lat_b + n_ctx_b
    n_sub = tr // HGRN_CHUNK

    def fwd_blk(s):
        return jnp.where(s < n_ctx_b, n_lat_b + s, s - n_ctx_b)

    def bwd_blk(s):
        return nb - 1 - s

    cq, ci, cff, cfb = OFF_BQ // D_B, OFF_BI // D_B, OFF_BFF // D_B, OFF_BFB // D_B
    full = lambda a: pl.BlockSpec(a.shape, lambda s: (0,) * a.ndim)
    kern = functools.partial(_hgrn_kernel, n_sub=n_sub)
    return pl.pallas_call(
        kern,
        grid=(nb,),
        in_specs=[
            pl.BlockSpec((None, None, 1, D_B), lambda s: (0, layer, 0, 0)),
            pl.BlockSpec((None, None, 1, D_B), lambda s: (1, layer, 0, 0)),
            full(cm), full(mk), full(rt),
            pl.BlockSpec((tr, D_B), lambda s: (fwd_blk(s), cq)),
            pl.BlockSpec((tr, D_B), lambda s: (fwd_blk(s), ci)),
            pl.BlockSpec((tr, D_B), lambda s: (fwd_blk(s), cff)),
            pl.BlockSpec((tr, D_B), lambda s: (bwd_blk(s), cq)),
            pl.BlockSpec((tr, D_B), lambda s: (bwd_blk(s), ci)),
            pl.BlockSpec((tr, D_B), lambda s: (bwd_blk(s), cfb)),
        ],
        out_specs=[
            pl.BlockSpec((tr, D_B), lambda s: (fwd_blk(s), 0)),
            pl.BlockSpec((tr, D_B), lambda s: (bwd_blk(s), 0)),
        ],
        out_shape=[jax.ShapeDtypeStruct((rows, D_B), F32)] * 2,
        scratch_shapes=[
            pltpu.VMEM((2, B_HEADS, B_DIM, B_DIM), F32),
            pltpu.VMEM((2, n_sub, (2 + HGRN_LEVELS) * HGRN_CHUNK, D_B), F32),
            pltpu.VMEM((2, tr, D_B), F32),
            pltpu.VMEM((2, tr, D_B), F32),
            pltpu.VMEM((2, n_sub, B_HEADS, HGRN_CHUNK, HGRN_CHUNK), BF16),
        ],
        compiler_params=_cparams(1),
    )(lb_all, lb_all, cm, mk, rt, p, p, p, p, p, p)


def _gelu(v):
    return 0.5 * v * (1.0 + lax.erf(v * (2.0 ** -0.5)))


def _out_kernel(x_ref, mod_ref, u_ref, v_ref, z_ref, bz_ref, of_ref, ob_ref, ycl_ref, ycc_ref,
                lnw_ref, lnb_ref, ws_ref, bs_ref, hw_ref, wout_ref, fw_ref,
                o_ref, y_ref, *, n_lat, tm, final):
    i = pl.program_id(0)
    d = x_ref.shape[1]
    n_lat_tiles = n_lat // tm

    def body(n, carry):
        rows = pl.ds(pl.multiple_of(n * A_CHUNK, A_CHUNK), A_CHUNK)
        u = _gelu(u_ref[rows, :].astype(F32))
        v = _gelu(v_ref[rows, :].astype(F32))
        mu = jnp.mean(v, axis=-1, keepdims=True)
        vc = v - mu
        var = jnp.mean(vc * vc, axis=-1, keepdims=True)
        vln = (vc * lax.rsqrt(var + NORM_EPS) * lnw_ref[...] + lnb_ref[...]).astype(BF16)
        gz = _silu(z_ref[rows, :].astype(F32))
        ob = of_ref[rows, :] + ob_ref[rows, :]
        gbz = _silu(bz_ref[rows, :].astype(F32))
        for h in range(A_HEADS):
            ln = slice(h * LANES, (h + 1) * LANES)
            s = jnp.dot(ws_ref[h], vln[:, ln], preferred_element_type=F32) + bs_ref[h]
            y_ref[rows, h * LANES:(h + 1) * LANES] = (u[:, ln] * s * gz[:, ln]).astype(BF16)
            oh = ob[:, ln]
            ms = jnp.mean(oh * oh, axis=-1, keepdims=True)
            yb = oh * lax.rsqrt(ms + NORM_EPS) * hw_ref[...] * gbz[:, ln]
            y_ref[rows, D_A + h * LANES:D_A + (h + 1) * LANES] = yb.astype(BF16)
        return carry

    lax.fori_loop(0, tm // A_CHUNK, body, 0)

    @pl.when(i < n_lat_tiles)
    def _():
        y_ref[:, D_A + D_B:] = ycl_ref[...]

    @pl.when(i >= n_lat_tiles)
    def _():
        y_ref[:, D_A + D_B:] = ycc_ref[...]

    out = jnp.dot(y_ref[...], wout_ref[...], preferred_element_type=F32)
    gate = jnp.where(i >= n_lat_tiles, mod_ref[1:2, 2 * d:3 * d], mod_ref[0:1, 2 * d:3 * d])
    xn = x_ref[...] + gate * out
    if final:
        ms = jnp.mean(xn * xn, axis=-1, keepdims=True)
        xn = xn * lax.rsqrt(ms + NORM_EPS) * fw_ref[...]
    o_ref[...] = xn


def _outproj(xa, mods, p, o_f, o_b, yc_lat, yc_ctx, ln_w, ln_b, ws_bf, bs_b, hnorm_w, w_out_bf, final_w,
             *, layer, n_lat, n_rows, tm, final):
    d = xa.shape[1]
    assert n_lat % tm == 0 and n_rows % tm == 0 and yc_ctx.shape[0] % tm == 0
    n_lat_tiles = n_lat // tm
    kern = functools.partial(_out_kernel, n_lat=n_lat, tm=tm, final=final)
    cu, cv, cz, cbz = OFF_AU // D_A, OFF_AV // D_A, OFF_AZ // D_A, OFF_BZ // D_B
    full = lambda a: pl.BlockSpec(a.shape, lambda i: (0,) * a.ndim)
    layered = lambda a: pl.BlockSpec((None,) + a.shape[1:], lambda i: (layer,) + (0,) * (a.ndim - 1))
    return pl.pallas_call(
        kern,
        grid=(n_rows // tm,),
        in_specs=[
            pl.BlockSpec((tm, d), lambda i: (i, 0)),
            layered(mods),
            pl.BlockSpec((tm, D_A), lambda i: (i, cu)),
            pl.BlockSpec((tm, D_A), lambda i: (i, cv)),
            pl.BlockSpec((tm, D_A), lambda i: (i, cz)),
            pl.BlockSpec((tm, D_B), lambda i: (i, cbz)),
            pl.BlockSpec((tm, D_B), lambda i: (i, 0)),
            pl.BlockSpec((tm, D_B), lambda i: (i, 0)),
            pl.BlockSpec((tm, D_C), lambda i: (jnp.minimum(i, n_lat_tiles - 1), 0)),
            pl.BlockSpec((tm, D_C), lambda i: (jnp.maximum(i - n_lat_tiles, 0), 0)),
            layered(ln_w), layered(ln_b), layered(ws_bf), layered(bs_b), layered(hnorm_w), layered(w_out_bf),
            full(final_w),
        ],
        out_specs=pl.BlockSpec((tm, d), lambda i: (i, 0)),
        out_shape=jax.ShapeDtypeStruct((n_rows, d), F32),
        scratch_shapes=[pltpu.VMEM((tm, d), BF16)],
        compiler_params=_cparams(1),
    )(xa, mods, p, p, p, p, o_f, o_b, yc_lat, yc_ctx, ln_w, ln_b, ws_bf, bs_b, hnorm_w, w_out_bf, final_w)


def _rope_tables(n_lat, n_ctx):
    rows = n_lat // GRID_W
    row_ids = jnp.repeat(jnp.arange(rows, dtype=F32), GRID_W)
    col_ids = jnp.tile(jnp.arange(GRID_W, dtype=F32), rows)
    inv = ROPE_THETA ** (-jnp.arange(0, ROPE_AXIS_DIM, 2, dtype=F32) / ROPE_AXIS_DIM)
    ang_r = row_ids[:, None] * inv[None, :]
    ang_c = col_ids[:, None] * inv[None, :]
    zero = jnp.zeros_like(ang_r)
    cos_t = jnp.concatenate([jnp.cos(ang_r), jnp.cos(ang_r), jnp.cos(ang_c), jnp.cos(ang_c)], axis=1)
    sa_t = jnp.concatenate([zero, jnp.sin(ang_r), zero, jnp.sin(ang_c)], axis=1)
    sb_t = jnp.concatenate([-jnp.sin(ang_r), zero, -jnp.sin(ang_c), zero], axis=1)
    pad = lambda a, v: jnp.concatenate([a, jnp.full((n_ctx, LANES), v, F32)], axis=0)
    return pad(cos_t, 1.0), pad(sa_t, 0.0), pad(sb_t, 0.0)


def _pick_tile(n, candidates):
    for t in candidates:
        if n % t == 0:
            return t
    raise ValueError(f"no tile for {n}")


def kernel(x, c, ctx, c_ctx, ada_w, ada_b, norm_w, w_in, gmlp_ln_w, gmlp_ln_b, gmlp_ws, gmlp_bs,
           hgrn_lower_bounds, hgrn_norm_w, diff_lambda, diff_subln_w, w_out, final_norm_w):
    bsz, n_lat, d = x.shape
    n_ctx = ctx.shape[1]
    depth = ada_w.shape[0]
    assert bsz == 1 and d == D_MODEL
    assert n_lat % 512 == 0 and n_ctx % 256 == 0 and n_lat % GRID_W == 0
    n_rows = n_lat + n_ctx

    xa = jnp.concatenate([x[0], ctx[0]], axis=0)
    cc = jnp.zeros((8, d), F32).at[0].set(c[0]).at[1].set(c_ctx)
    mods = _adaln(cc, ada_w, ada_b)
    lb_all = _lower_bounds(hgrn_lower_bounds).reshape(2, depth, 1, D_B)
    cos_t, sa_t, sb_t = _rope_tables(n_lat, n_ctx)
    cm, mk, rt = _hgrn_constants()
    consts = (jnp.asarray(cm, BF16), jnp.asarray(mk), jnp.asarray(rt))

    w_in_bf = w_in.astype(BF16)
    w_out_bf = w_out.astype(BF16)
    ws_bf = gmlp_ws.astype(BF16)
    bs_b = jnp.broadcast_to(gmlp_bs[:, :, :, None], gmlp_bs.shape + (LANES,))
    norm_w3 = norm_w.reshape(depth, 1, d)
    ln_w3 = gmlp_ln_w.reshape(depth, 1, D_A)
    ln_b3 = gmlp_ln_b.reshape(depth, 1, D_A)
    hnorm_w3 = hgrn_norm_w.reshape(depth, 1, B_DIM)
    subln_w3 = diff_subln_w.reshape(depth, 1, C_VAL_DIM)
    final_w2 = final_norm_w.reshape(1, d)

    tm_in = _pick_tile(n_rows, (768, 256))
    tm_out = _pick_tile(n_ctx, (256,))
    tm_fin = _pick_tile(n_lat, (512,))
    tq = _pick_tile(n_lat, (512,))
    tk = 2048 if n_lat >= 8192 else 512

    for layer in range(depth):
        lam_init = 0.8 - 0.6 * math.exp(-0.3 * layer)
        last = layer == depth - 1
        p = _inproj(xa, mods, norm_w3, cos_t, sa_t, sb_t, w_in_bf,
                    layer=layer, n_lat=n_lat, tm=tm_in, tn=1024)
        yc_lat = _attention(p, diff_lambda, subln_w3, layer=layer, q_row0=0, n_q=n_lat,
                            kv_row0=0, n_kv=n_rows, tq=tq, tk=tk, lam_init=lam_init)
        if last:
            yc_ctx = yc_lat
        else:
            yc_ctx = _attention(p, diff_lambda, subln_w3, layer=layer, q_row0=n_lat, n_q=n_ctx,
                                kv_row0=n_lat, n_kv=n_ctx, tq=n_ctx, tk=n_ctx, lam_init=lam_init)
        o_f, o_b = _hgrn(p, lb_all, consts, layer=layer, n_lat=n_lat, n_ctx=n_ctx, tr=256)
        xa = _outproj(xa, mods, p, o_f, o_b, yc_lat, yc_ctx, ln_w3, ln_b3, ws_bf, bs_b, hnorm_w3,
                      w_out_bf, final_w2, layer=layer, n_lat=n_lat,
                      n_rows=n_lat if last else n_rows, tm=tm_fin if last else tm_out, final=last)
    return xa[None]
```

```python
import functools
import math

import numpy as np
import jax
import jax.numpy as jnp
from jax import lax
from jax.experimental import pallas as pl
from jax.experimental.pallas import tpu as pltpu

F32 = jnp.float32
BF16 = jnp.bfloat16

D_MODEL = 2048
GRID_W = 64
NORM_EPS = 1e-6
D_A = 512
A_HEADS = 4
A_CHUNK = 128
D_B = 512
B_HEADS = 4
B_DIM = 128
D_C = 1024
C_HEADS = 4
C_HEAD_DIM = 128
C_VAL_DIM = 256
ROPE_THETA = 10000.0
ROPE_AXIS_DIM = 64
D_IN = 3 * D_A + 5 * D_B + 4 * D_C
OFF_AU, OFF_AV, OFF_AZ = 0, 512, 1024
OFF_BQ, OFF_BI, OFF_BFF, OFF_BFB, OFF_BZ = 1536, 2048, 2560, 3072, 3584
OFF_CQ, OFF_CK, OFF_CV, OFF_CZ = 4096, 5120, 6144, 7168
Q_SCALE = C_HEAD_DIM ** -0.5 * math.log2(math.e)

LANES = 128
HGRN_CHUNK = 64
HGRN_LEVELS = 6
LOG2_E = math.log2(math.e)
ATTN_ROW_BLOCK = 32
VMEM_LIMIT = 56 * 1024 * 1024


def _cparams(n_axes):
    return pltpu.CompilerParams(
        dimension_semantics=("arbitrary",) * n_axes,
        vmem_limit_bytes=VMEM_LIMIT,
    )


def _silu(v):
    return v * jax.nn.sigmoid(v)


def _nt_dot(a, b):
    return lax.dot_general(a, b, (((1,), (1,)), ((), ())), preferred_element_type=F32)


def _adaln_kernel(c_ref, w_ref, b_ref, o_ref):
    s = _silu(c_ref[...])
    o_ref[...] = (
        jnp.dot(s, w_ref[...], precision=lax.Precision.HIGHEST, preferred_element_type=F32)
        + b_ref[...]
    )


def _adaln(cc, ada_w, ada_b):
    depth, d, d3 = ada_w.shape
    tn = 1024
    return pl.pallas_call(
        _adaln_kernel,
        grid=(depth, d3 // tn),
        in_specs=[
            pl.BlockSpec((8, d), lambda l, j: (0, 0)),
            pl.BlockSpec((None, d, tn), lambda l, j: (l, 0, j)),
            pl.BlockSpec((None, 1, tn), lambda l, j: (l, 0, j)),
        ],
        out_specs=pl.BlockSpec((None, 8, tn), lambda l, j: (l, 0, j)),
        out_shape=jax.ShapeDtypeStruct((depth, 8, d3), F32),
        compiler_params=_cparams(2),
    )(cc, ada_w, ada_b.reshape(depth, 1, d3))


def _lower_bound_kernel(x_ref, o_ref):
    depth = x_ref.shape[1]
    for d in range(2):
        x = x_ref[d]
        e = jnp.exp(x - jnp.max(x, axis=0, keepdims=True))
        soft = e / jnp.sum(e, axis=0, keepdims=True)
        run = jnp.zeros((1, x.shape[1]), F32)
        o_ref[d, 0:1, :] = run
        for l in range(1, depth):
            run = run + soft[l:l + 1, :]
            o_ref[d, l:l + 1, :] = run


def _lower_bounds(hgrn_lower_bounds):
    return pl.pallas_call(
        _lower_bound_kernel,
        out_shape=jax.ShapeDtypeStruct(hgrn_lower_bounds.shape, F32),
    )(hgrn_lower_bounds.astype(F32))


def _inproj_kernel(x_ref, mod_ref, nw_ref, cos_ref, sa_ref, sb_ref, w_ref, o_ref, h_ref,
                   *, n_lat, tm, tn, sub):
    i = pl.program_id(0)
    j = pl.program_id(1)
    d = x_ref.shape[1]

    @pl.when(j == 0)
    def _():
        nw = nw_ref[...]
        w_lat = nw * (1.0 + mod_ref[0:1, d:2 * d])
        w_ctx = nw * (1.0 + mod_ref[1:2, d:2 * d])

        def body(r, carry):
            rows = pl.ds(pl.multiple_of(r * sub, sub), sub)
            is_ctx = i * tm + r * sub >= n_lat
            w_eff = jnp.where(is_ctx, w_ctx, w_lat)
            shift = jnp.where(is_ctx, mod_ref[1:2, 0:d], mod_ref[0:1, 0:d])
            x = x_ref[rows, :]
            ms = jnp.mean(x * x, axis=-1, keepdims=True)
            h_ref[rows, :] = (x * lax.rsqrt(ms + NORM_EPS) * w_eff + shift).astype(BF16)
            return carry

        lax.fori_loop(0, tm // sub, body, 0, unroll=4)

    is_rope = j * tn == OFF_CQ
    halves = [slice(k * D_C, (k + 1) * D_C) for k in range(tn // D_C)]

    @pl.when(jnp.logical_not(is_rope))
    def _():
        for cs in halves:
            o_ref[:, cs] = jnp.dot(h_ref[...], w_ref[:, cs], preferred_element_type=F32).astype(o_ref.dtype)

    @pl.when(is_rope)
    def _():
        for k, cs in enumerate(halves):
            acc = jnp.dot(h_ref[...], w_ref[:, cs], preferred_element_type=F32)
            mult = Q_SCALE if k == 0 else 1.0
            c = cos_ref[...] * mult
            sa = sa_ref[...] * mult
            sb = sb_ref[...] * mult
            for g in range(D_C // LANES):
                xg = acc[:, g * LANES:(g + 1) * LANES]
                yg = xg * c + pltpu.roll(xg, 32, 1) * sa + pltpu.roll(xg, LANES - 32, 1) * sb
                o_ref[:, k * D_C + g * LANES:k * D_C + (g + 1) * LANES] = yg.astype(o_ref.dtype)


def _inproj(xa, mods, norm_w, cos_t, sa_t, sb_t, w_in_bf, *, layer, n_lat, tm, tn):
    rows, d = xa.shape
    d_in = w_in_bf.shape[2]
    sub = 32
    assert tn == OFF_CV - OFF_CQ == 2 * D_C and OFF_CQ % tn == 0 and n_lat % sub == 0 and tm % (4 * sub) == 0
    kern = functools.partial(_inproj_kernel, n_lat=n_lat, tm=tm, tn=tn, sub=sub)
    return pl.pallas_call(
        kern,
        grid=(rows // tm, d_in // tn),
        in_specs=[
            pl.BlockSpec((tm, d), lambda i, j: (i, 0)),
            pl.BlockSpec((None,) + mods.shape[1:], lambda i, j: (layer, 0, 0)),
            pl.BlockSpec((None, 1, d), lambda i, j: (layer, 0, 0)),
            pl.BlockSpec((tm, LANES), lambda i, j: (i, 0)),
            pl.BlockSpec((tm, LANES), lambda i, j: (i, 0)),
            pl.BlockSpec((tm, LANES), lambda i, j: (i, 0)),
            pl.BlockSpec((None, d, tn), lambda i, j: (layer, 0, j)),
        ],
        out_specs=pl.BlockSpec((tm, tn), lambda i, j: (i, j)),
        out_shape=jax.ShapeDtypeStruct((rows, d_in), BF16),
        scratch_shapes=[pltpu.VMEM((tm, d), BF16)],
        compiler_params=_cparams(2),
    )(xa, mods, norm_w, cos_t, sa_t, sb_t, w_in_bf)


def _attn_kernel(lam_ref, sw_ref, q_ref, k_ref, v_ref, z_ref,
                 o_ref, m_ref, l_ref, acc_ref, sa_ref, sb_ref, mpa_ref, mpb_ref, p_ref, al_ref,
                 *, chunks, lam_init, rb):
    tq = q_ref.shape[0]
    m_ref[...] = jnp.full(m_ref.shape, -jnp.inf, F32)
    l_ref[...] = jnp.zeros(l_ref.shape, F32)
    acc_ref[...] = jnp.zeros(acc_ref.shape, F32)
    cw = 2 * LANES

    def scores(row0, size, s_ref, mp_ref):
        for u in range(2):
            q = q_ref[:, u * C_HEAD_DIM:(u + 1) * C_HEAD_DIM]
            mp = None
            for cb in range(size // cw):
                k = k_ref[pl.ds(row0 + cb * cw, cw), u * C_HEAD_DIM:(u + 1) * C_HEAD_DIM]
                sc = _nt_dot(q, k)
                s_ref[u, :, cb * cw:(cb + 1) * cw] = sc
                part = jnp.maximum(sc[:, 0:LANES], sc[:, LANES:cw])
                mp = part if mp is None else jnp.maximum(mp, part)
            mp_ref[u] = mp

    def softmax_pv(row0, size, s_ref, mp_ref):
        v = v_ref[pl.ds(row0, size), :]
        for u in range(2):
            for r0 in range(0, tq, rb):
                r = slice(r0, r0 + rb)
                m_prev = m_ref[u, r, :]
                m_new = jnp.maximum(m_prev, jnp.max(mp_ref[u, r, :], axis=-1, keepdims=True))
                alpha = jnp.exp2(m_prev - m_new)
                p = jnp.exp2(s_ref[u, r, 0:size] - m_new)
                lp = p[:, 0:LANES]
                for g in range(1, size // LANES):
                    lp = lp + p[:, g * LANES:(g + 1) * LANES]
                l_ref[u, r, :] = alpha * l_ref[u, r, :] + lp
                p_ref[u, r, 0:size] = p.astype(BF16)
                al_ref[u, r, :] = alpha
                m_ref[u, r, :] = m_new
            acc_ref[u] = al_ref[u] * acc_ref[u] + jnp.dot(p_ref[u, :, 0:size], v, preferred_element_type=F32)

    slots = ((sa_ref, mpa_ref), (sb_ref, mpb_ref))
    row0 = 0
    starts = []
    for size in chunks:
        starts.append(row0)
        row0 += size
    scores(starts[0], chunks[0], *slots[0])
    for c, size in enumerate(chunks):
        if c + 1 < len(chunks):
            scores(starts[c + 1], chunks[c + 1], *slots[(c + 1) % 2])
        softmax_pv(starts[c], size, *slots[c % 2])

    lp = lam_ref[...]
    lam = (jnp.exp(jnp.sum(lp[0:1] * lp[1:2], axis=-1, keepdims=True))
           - jnp.exp(jnp.sum(lp[2:3] * lp[3:4], axis=-1, keepdims=True)) + lam_init)
    l0 = jnp.sum(l_ref[0], axis=-1, keepdims=True)
    l1 = jnp.sum(l_ref[1], axis=-1, keepdims=True)
    o = acc_ref[0] / l0 - lam * (acc_ref[1] / l1)
    ms = jnp.mean(o * o, axis=-1, keepdims=True)
    y = o * lax.rsqrt(ms + NORM_EPS) * sw_ref[...] * (1.0 - lam_init)
    o_ref[...] = (y * _silu(z_ref[...].astype(F32))).astype(o_ref.dtype)


def _key_chunks(n_kv, tk):
    unit = 2 * LANES
    assert n_kv % unit == 0
    sizes = [tk] * (n_kv // tk)
    if n_kv % tk:
        sizes.append(n_kv % tk)
    return tuple(sizes)


def _attention(p, lam_p, subln_w, *, layer, q_row0, n_q, kv_row0, n_kv, tq, tk, lam_init):
    qb0, kb0 = q_row0 // tq, kv_row0 // n_kv
    chunks = _key_chunks(n_kv, tk)
    ts = max(chunks)
    cq, ck, cv, cz = (OFF_CQ // C_VAL_DIM, OFF_CK // C_VAL_DIM, OFF_CV // C_VAL_DIM, OFF_CZ // C_VAL_DIM)
    kern = functools.partial(_attn_kernel, chunks=chunks, lam_init=lam_init, rb=ATTN_ROW_BLOCK)
    in_specs = [
        pl.BlockSpec((None, 4, C_HEAD_DIM), lambda h, i: (layer, 0, 0)),
        pl.BlockSpec((None, 1, C_VAL_DIM), lambda h, i: (layer, 0, 0)),
        pl.BlockSpec((tq, C_VAL_DIM), lambda h, i: (qb0 + i, cq + h)),
        pl.BlockSpec((n_kv, C_VAL_DIM), lambda h, i: (kb0, ck + h)),
        pl.BlockSpec((n_kv, C_VAL_DIM), lambda h, i: (kb0, cv + h)),
        pl.BlockSpec((tq, C_VAL_DIM), lambda h, i: (qb0 + i, cz + h)),
    ]
    return pl.pallas_call(
        kern,
        grid=(C_HEADS, n_q // tq),
        in_specs=in_specs,
        out_specs=pl.BlockSpec((tq, C_VAL_DIM), lambda h, i: (i, h)),
        out_shape=jax.ShapeDtypeStruct((n_q, D_C), BF16),
        scratch_shapes=[
            pltpu.VMEM((2, tq, 1), F32),
            pltpu.VMEM((2, tq, LANES), F32),
            pltpu.VMEM((2, tq, C_VAL_DIM), F32),
            pltpu.VMEM((2, tq, ts), F32),
            pltpu.VMEM((2, tq, ts), F32),
            pltpu.VMEM((2, tq, LANES), F32),
            pltpu.VMEM((2, tq, LANES), F32),
            pltpu.VMEM((2, tq, ts), BF16),
            pltpu.VMEM((2, tq, 1), F32),
        ],
        compiler_params=_cparams(2),
    )(lam_p, subln_w, p, p, p, p)


def _hgrn_constants():
    c = HGRN_CHUNK
    t = np.arange(c)
    mats = [(t[None, :] <= t[:, None]), (t[None, :] > t[:, None])]
    masks, rts = [], []
    for lvl in range(HGRN_LEVELS):
        n = (c // 2) >> lvl
        later = (t & n) != 0
        parent = t // (2 * n)
        mid = parent * 2 * n + n - 1
        dm = np.zeros((c, c), bool)
        for tt in range(c):
            if later[tt]:
                dm[tt, mid[tt] + 1:tt + 1] = True
            else:
                dm[tt, tt + 1:mid[tt] + 1] = True
        mats.append(dm)
        masks.append(later[:, None] & (~later[None, :]) & (parent[:, None] == parent[None, :]))
        rts.append(later)
    masks.append(np.eye(c, dtype=bool))
    cm = np.concatenate(mats, axis=0).astype(np.float32)
    mk = np.stack(masks).astype(np.float32)
    rt = np.repeat(np.stack(rts).astype(np.float32)[:, :, None], LANES, axis=2)

    def flip(a):
        blocks = a.reshape(-1, c, a.shape[-1])
        if a.shape[-1] == c:
            blocks = blocks[:, ::-1, ::-1]
        else:
            blocks = blocks[:, ::-1, :]
        return blocks.reshape(a.shape)

    cm2 = np.stack([cm, flip(cm)])
    cm2 = np.concatenate([cm2, cm2, cm2], axis=2)
    mk2 =np.stack([mk, flip(mk.reshape(-1, c)).reshape(mk.shape)])
    rt2 = np.stack([rt, flip(rt.reshape(-1, LANES)).reshape(rt.shape)])
    return cm2, mk2, rt2


def _hgrn_kernel(lbf_ref, lbb_ref, cm_ref, mk_ref, rt_ref,
                 qf_ref, vf_ref, af_ref, qb_ref, vb_ref, ab_ref,
                 of_ref, ob_ref, st_ref, f_ref, qs_ref, kk_ref, sc_ref, *, n_sub):
    c = HGRN_CHUNK

    @pl.when(pl.program_id(0) == 0)
    def _():
        st_ref[...] = jnp.zeros(st_ref.shape, F32)

    dirs = ((0, af_ref, qf_ref, vf_ref, lbf_ref, of_ref), (1, ab_ref, qb_ref, vb_ref, lbb_ref, ob_ref))

    def chunk_rows(d, t):
        blk = t if d == 0 else n_sub - 1 - t
        return slice(blk * c, (blk + 1) * c)

    for d, a_ref, q_ref, v_ref, lb_ref, o_ref in dirs:
        a = a_ref[...].astype(F32)
        lb = lb_ref[...]
        log_lb = jnp.log(lb)
        log_1m = jnp.log1p(-lb)
        log_sig = jnp.minimum(a, 0.0) - jnp.log(1.0 + jnp.exp(-jnp.abs(a)))
        q2 = log_1m + log_sig
        g = jnp.maximum(log_lb, q2) + jnp.log(1.0 + jnp.exp(-jnp.abs(log_lb - q2)))
        g = g * LOG2_E
        kk_ref[d] = (1.0 - lb) * jax.nn.sigmoid(-a)
        qs_ref[d] = _silu(q_ref[...].astype(F32))
        g1 = g.astype(BF16)
        r1 = g - g1.astype(F32)
        g2 = r1.astype(BF16)
        g3 = (r1 - g2.astype(F32)).astype(BF16)
        for t in range(n_sub):
            r = chunk_rows(d, t)
            gcat = jnp.concatenate([g1[r], g2[r], g3[r]], axis=0)
            f_ref[d, t] = jnp.exp2(jnp.dot(cm_ref[d], gcat, preferred_element_type=F32))

    for d, a_ref, q_ref, v_ref, lb_ref, o_ref in dirs:
        q_side = [rt_ref[d, lvl] != 0.0 for lvl in range(HGRN_LEVELS)]
        for t in range(n_sub):
            r = chunk_rows(d, t)
            for h in range(B_HEADS):
                ln = slice(h * B_DIM, (h + 1) * B_DIM)
                qs = qs_ref[d, r, ln]
                kk = kk_ref[d, r, ln]
                sc = mk_ref[d, HGRN_LEVELS] * _nt_dot(qs.astype(BF16), kk.astype(BF16))
                for lvl in range(HGRN_LEVELS):
                    fl = f_ref[d, t, (2 + lvl) * c:(3 + lvl) * c, ln]
                    xl = (jnp.where(q_side[lvl], qs, kk) * fl).astype(BF16)
                    sc = sc + mk_ref[d, lvl] * _nt_dot(xl, xl)
                sc_ref[d, t, h] = sc.astype(BF16)

    for t in range(n_sub):
        for d, a_ref, q_ref, v_ref, lb_ref, o_ref in dirs:
            r = chunk_rows(d, t)
            last = c - 1 if d == 0 else 0
            for h in range(B_HEADS):
                ln = slice(h * B_DIM, (h + 1) * B_DIM)
                v32 = v_ref[r, ln].astype(F32)
                qe = (qs_ref[d, r, ln] * f_ref[d, t, 0:c, ln]).astype(BF16)
                kdec = (kk_ref[d, r, ln] * f_ref[d, t, c:2 * c, ln]).astype(BF16)
                e_last = f_ref[d, t, last:last + 1, ln]
                st = st_ref[d, h]
                o = (jnp.dot(sc_ref[d, t, h], v32.astype(BF16), preferred_element_type=F32)
                     + _nt_dot(qe, st.astype(BF16)))
                st_ref[d, h] = st * e_last + jnp.dot(v32.T.astype(BF16), kdec, preferred_element_type=F32)
                o_ref[r, ln] = o.astype(o_ref.dtype)


def _hgrn(p, lb_all, consts, *, layer, n_lat, n_ctx, tr):
    rows = p.shape[0]
    cm, mk, rt = consts
    n_lat_b, n_ctx_b = n_lat // tr, n_ctx // tr
    nb = n_lat_b + n_ctx_b
    n_sub = tr // HGRN_CHUNK

    def fwd_blk(s):
        return jnp.where(s < n_ctx_b, n_lat_b + s, s - n_ctx_b)

    def bwd_blk(s):
        return nb - 1 - s

    cq, ci, cff, cfb = OFF_BQ // D_B, OFF_BI // D_B, OFF_BFF // D_B, OFF_BFB // D_B
    full = lambda a: pl.BlockSpec(a.shape, lambda s: (0,) * a.ndim)
    kern = functools.partial(_hgrn_kernel, n_sub=n_sub)
    return pl.pallas_call(
        kern,
        grid=(nb,),
        in_specs=[
            pl.BlockSpec((None, None, 1, D_B), lambda s: (0, layer, 0, 0)),
            pl.BlockSpec((None, None, 1, D_B), lambda s: (1, layer, 0, 0)),
            full(cm), full(mk), full(rt),
            pl.BlockSpec((tr, D_B), lambda s: (fwd_blk(s), cq)),
            pl.BlockSpec((tr, D_B), lambda s: (fwd_blk(s), ci)),
            pl.BlockSpec((tr, D_B), lambda s: (fwd_blk(s), cff)),
            pl.BlockSpec((tr, D_B), lambda s: (bwd_blk(s), cq)),
            pl.BlockSpec((tr, D_B), lambda s: (bwd_blk(s), ci)),
            pl.BlockSpec((tr, D_B), lambda s: (bwd_blk(s), cfb)),
        ],
        out_specs=[
            pl.BlockSpec((tr, D_B), lambda s: (fwd_blk(s), 0)),
            pl.BlockSpec((tr, D_B), lambda s: (bwd_blk(s), 0)),
        ],
        out_shape=[jax.ShapeDtypeStruct((rows, D_B), F32)] * 2,
        scratch_shapes=[
            pltpu.VMEM((2, B_HEADS, B_DIM, B_DIM), F32),
            pltpu.VMEM((2, n_sub, (2 + HGRN_LEVELS) * HGRN_CHUNK, D_B), F32),
            pltpu.VMEM((2, tr, D_B), F32),
            pltpu.VMEM((2, tr, D_B), F32),
            pltpu.VMEM((2, n_sub, B_HEADS, HGRN_CHUNK, HGRN_CHUNK), BF16),
        ],
        compiler_params=_cparams(1),
    )(lb_all, lb_all, cm, mk, rt, p, p, p, p, p, p)


def _gelu(v):
    return 0.5 * v * (1.0 + lax.erf(v * (2.0 ** -0.5)))


def _out_kernel(x_ref, mod_ref, u_ref, v_ref, z_ref, bz_ref, of_ref, ob_ref, ycl_ref, ycc_ref,
                lnw_ref, lnb_ref, ws_ref, bs_ref, hw_ref, wout_ref, fw_ref,
                o_ref, y_ref, *, n_lat, tm, final):
    i = pl.program_id(0)
    d = x_ref.shape[1]
    n_lat_tiles = n_lat // tm

    def body(n, carry):
        rows = pl.ds(pl.multiple_of(n * A_CHUNK, A_CHUNK), A_CHUNK)
        u = _gelu(u_ref[rows, :].astype(F32))
        v = _gelu(v_ref[rows, :].astype(F32))
        mu = jnp.mean(v, axis=-1, keepdims=True)
        vc = v - mu
        var = jnp.mean(vc * vc, axis=-1, keepdims=True)
        vln = (vc * lax.rsqrt(var + NORM_EPS) * lnw_ref[...] + lnb_ref[...]).astype(BF16)
        gz = _silu(z_ref[rows, :].astype(F32))
        ob = of_ref[rows, :] + ob_ref[rows, :]
        gbz = _silu(bz_ref[rows, :].astype(F32))
        for h in range(A_HEADS):
            ln = slice(h * LANES, (h + 1) * LANES)
            s = jnp.dot(ws_ref[h], vln[:, ln], preferred_element_type=F32) + bs_ref[h]
            y_ref[rows, h * LANES:(h + 1) * LANES] = (u[:, ln] * s * gz[:, ln]).astype(BF16)
            oh = ob[:, ln]
            ms = jnp.mean(oh * oh, axis=-1, keepdims=True)
            yb = oh * lax.rsqrt(ms + NORM_EPS) * hw_ref[...] * gbz[:, ln]
            y_ref[rows, D_A + h * LANES:D_A + (h + 1) * LANES] = yb.astype(BF16)
        return carry

    lax.fori_loop(0, tm // A_CHUNK, body, 0)

    @pl.when(i < n_lat_tiles)
    def _():
        y_ref[:, D_A + D_B:] = ycl_ref[...]

    @pl.when(i >= n_lat_tiles)
    def _():
        y_ref[:, D_A + D_B:] = ycc_ref[...]

    out = jnp.dot(y_ref[...], wout_ref[...], preferred_element_type=F32)
    gate = jnp.where(i >= n_lat_tiles, mod_ref[1:2, 2 * d:3 * d], mod_ref[0:1, 2 * d:3 * d])
    xn = x_ref[...] + gate * out
    if final:
        ms = jnp.mean(xn * xn, axis=-1, keepdims=True)
        xn = xn * lax.rsqrt(ms + NORM_EPS) * fw_ref[...]
    o_ref[...] = xn


def _outproj(xa, mods, p, o_f, o_b, yc_lat, yc_ctx, ln_w, ln_b, ws_bf, bs_b, hnorm_w, w_out_bf, final_w,
             *, layer, n_lat, n_rows, tm, final):
    d = xa.shape[1]
    assert n_lat % tm == 0 and n_rows % tm == 0 and yc_ctx.shape[0] % tm == 0
    n_lat_tiles = n_lat // tm
    kern = functools.partial(_out_kernel, n_lat=n_lat, tm=tm, final=final)
    cu, cv, cz, cbz = OFF_AU // D_A, OFF_AV // D_A, OFF_AZ // D_A, OFF_BZ // D_B
    full = lambda a: pl.BlockSpec(a.shape, lambda i: (0,) * a.ndim)
    layered = lambda a: pl.BlockSpec((None,) + a.shape[1:], lambda i: (layer,) + (0,) * (a.ndim - 1))
    return pl.pallas_call(
        kern,
        grid=(n_rows // tm,),
        in_specs=[
            pl.BlockSpec((tm, d), lambda i: (i, 0)),
            layered(mods),
            pl.BlockSpec((tm, D_A), lambda i: (i, cu)),
            pl.BlockSpec((tm, D_A), lambda i: (i, cv)),
            pl.BlockSpec((tm, D_A), lambda i: (i, cz)),
            pl.BlockSpec((tm, D_B), lambda i: (i, cbz)),
            pl.BlockSpec((tm, D_B), lambda i: (i, 0)),
            pl.BlockSpec((tm, D_B), lambda i: (i, 0)),
            pl.BlockSpec((tm, D_C), lambda i: (jnp.minimum(i, n_lat_tiles - 1), 0)),
            pl.BlockSpec((tm, D_C), lambda i: (jnp.maximum(i - n_lat_tiles, 0), 0)),
            layered(ln_w), layered(ln_b), layered(ws_bf), layered(bs_b), layered(hnorm_w), layered(w_out_bf),
            full(final_w),
        ],
        out_specs=pl.BlockSpec((tm, d), lambda i: (i, 0)),
        out_shape=jax.ShapeDtypeStruct((n_rows, d), F32),
        scratch_shapes=[pltpu.VMEM((tm, d), BF16)],
        compiler_params=_cparams(1),
    )(xa, mods, p, p, p, p, o_f, o_b, yc_lat, yc_ctx, ln_w, ln_b, ws_bf, bs_b, hnorm_w, w_out_bf, final_w)


def _rope_tables(n_lat, n_ctx):
    rows = n_lat // GRID_W
    row_ids = jnp.repeat(jnp.arange(rows, dtype=F32), GRID_W)
    col_ids = jnp.tile(jnp.arange(GRID_W, dtype=F32), rows)
    inv = ROPE_THETA ** (-jnp.arange(0, ROPE_AXIS_DIM, 2, dtype=F32) / ROPE_AXIS_DIM)
    ang_r = row_ids[:, None] * inv[None, :]
    ang_c = col_ids[:, None] * inv[None, :]
    zero = jnp.zeros_like(ang_r)
    cos_t = jnp.concatenate([jnp.cos(ang_r), jnp.cos(ang_r), jnp.cos(ang_c), jnp.cos(ang_c)], axis=1)
    sa_t = jnp.concatenate([zero, jnp.sin(ang_r), zero, jnp.sin(ang_c)], axis=1)
    sb_t = jnp.concatenate([-jnp.sin(ang_r), zero, -jnp.sin(ang_c), zero], axis=1)
    pad = lambda a, v: jnp.concatenate([a, jnp.full((n_ctx, LANES), v, F32)], axis=0)
    return pad(cos_t, 1.0), pad(sa_t, 0.0), pad(sb_t, 0.0)


def _pick_tile(n, candidates):
    for t in candidates:
        if n % t == 0:
            return t
    raise ValueError(f"no tile for {n}")


def kernel(x, c, ctx, c_ctx, ada_w, ada_b, norm_w, w_in, gmlp_ln_w, gmlp_ln_b, gmlp_ws, gmlp_bs,
           hgrn_lower_bounds, hgrn_norm_w, diff_lambda, diff_subln_w, w_out, final_norm_w):
    bsz, n_lat, d = x.shape
    n_ctx = ctx.shape[1]
    depth = ada_w.shape[0]
    assert bsz == 1 and d == D_MODEL
    assert n_lat % 512 == 0 and n_ctx % 256 == 0 and n_lat % GRID_W == 0
    n_rows = n_lat + n_ctx

    xa = jnp.concatenate([x[0], ctx[0]], axis=0)
    cc = jnp.zeros((8, d), F32).at[0].set(c[0]).at[1].set(c_ctx)
    mods = _adaln(cc, ada_w, ada_b)
    lb_all = _lower_bounds(hgrn_lower_bounds).reshape(2, depth, 1, D_B)
    cos_t, sa_t, sb_t = _rope_tables(n_lat, n_ctx)
    cm, mk, rt = _hgrn_constants()
    consts = (jnp.asarray(cm, BF16), jnp.asarray(mk), jnp.asarray(rt))

    w_in_bf = w_in.astype(BF16)
    w_out_bf = w_out.astype(BF16)
    ws_bf = gmlp_ws.astype(BF16)
    bs_b = jnp.broadcast_to(gmlp_bs[:, :, :, None], gmlp_bs.shape + (LANES,))
    norm_w3 = norm_w.reshape(depth, 1, d)
    ln_w3 = gmlp_ln_w.reshape(depth, 1, D_A)
    ln_b3 = gmlp_ln_b.reshape(depth, 1, D_A)
    hnorm_w3 = hgrn_norm_w.reshape(depth, 1, B_DIM)
    subln_w3 = diff_subln_w.reshape(depth, 1, C_VAL_DIM)
    final_w2 = final_norm_w.reshape(1, d)

    tm_in = _pick_tile(n_rows, (768, 256))
    tm_out = _pick_tile(n_ctx, (256,))
    tm_fin = _pick_tile(n_lat, (512,))
    tq = _pick_tile(n_lat, (512,))
    tk = 2048 if n_lat >= 8192 else 512

    for layer in range(depth):
        lam_init = 0.8 - 0.6 * math.exp(-0.3 * layer)
        last = layer == depth - 1
        p = _inproj(xa, mods, norm_w3, cos_t, sa_t, sb_t, w_in_bf,
                    layer=layer, n_lat=n_lat, tm=tm_in, tn=2 * D_C)
        yc_lat = _attention(p, diff_lambda, subln_w3, layer=layer, q_row0=0, n_q=n_lat,
                            kv_row0=0, n_kv=n_rows, tq=tq, tk=tk, lam_init=lam_init)
        if last:
            yc_ctx = yc_lat
        else:
            yc_ctx = _attention(p, diff_lambda, subln_w3, layer=layer, q_row0=n_lat, n_q=n_ctx,
                                kv_row0=n_lat, n_kv=n_ctx, tq=n_ctx, tk=n_ctx, lam_init=lam_init)
        o_f, o_b = _hgrn(p, lb_all, consts, layer=layer, n_lat=n_lat, n_ctx=n_ctx, tr=256)
        xa = _outproj(xa, mods, p, o_f, o_b, yc_lat, yc_ctx, ln_w3, ln_b3, ws_bf, bs_b, hnorm_w3,
                      w_out_bf, final_w2, layer=layer, n_lat=n_lat,
                      n_rows=n_lat if last else n_rows, tm=tm_fin if last else tm_out, final=last)
    return xa[None]
```

```python
import functools
import math

import numpy as np
import jax
import jax.numpy as jnp
from jax import lax
from jax.experimental import pallas as pl
from jax.experimental.pallas import tpu as pltpu

F32 = jnp.float32
BF16 = jnp.bfloat16

D_MODEL = 2048
GRID_W = 64
NORM_EPS = 1e-6
D_A = 512
A_HEADS = 4
A_CHUNK = 128
D_B = 512
B_HEADS = 4
B_DIM = 128
D_C = 1024
C_HEADS = 4
C_HEAD_DIM = 128
C_VAL_DIM = 256
ROPE_THETA = 10000.0
ROPE_AXIS_DIM = 64
D_IN = 3 * D_A + 5 * D_B + 4 * D_C
OFF_AU, OFF_AV, OFF_AZ = 0, 512, 1024
OFF_BQ, OFF_BI, OFF_BFF, OFF_BFB, OFF_BZ = 1536, 2048, 2560, 3072, 3584
OFF_CQ, OFF_CK, OFF_CV, OFF_CZ = 4096, 5120, 6144, 7168
Q_SCALE = C_HEAD_DIM ** -0.5 * math.log2(math.e)

LANES = 128
HGRN_CHUNK = 64
HGRN_LEVELS = 6
LOG2_E = math.log2(math.e)
ATTN_ROW_BLOCK = 32
VMEM_LIMIT = 56 * 1024 * 1024


def _cparams(n_axes):
    return pltpu.CompilerParams(
        dimension_semantics=("arbitrary",) * n_axes,
        vmem_limit_bytes=VMEM_LIMIT,
    )


def _silu(v):
    return v * jax.nn.sigmoid(v)


def _nt_dot(a, b):
    return lax.dot_general(a, b, (((1,), (1,)), ((), ())), preferred_element_type=F32)


def _adaln_kernel(c_ref, w_ref, b_ref, o_ref):
    s = _silu(c_ref[...])
    o_ref[...] = (
        jnp.dot(s, w_ref[...], precision=lax.Precision.HIGHEST, preferred_element_type=F32)
        + b_ref[...]
    )


def _adaln(cc, ada_w, ada_b):
    depth, d, d3 = ada_w.shape
    tn = 1024
    return pl.pallas_call(
        _adaln_kernel,
        grid=(depth, d3 // tn),
        in_specs=[
            pl.BlockSpec((8, d), lambda l, j: (0, 0)),
            pl.BlockSpec((None, d, tn), lambda l, j: (l, 0, j)),
            pl.BlockSpec((None, 1, tn), lambda l, j: (l, 0, j)),
        ],
        out_specs=pl.BlockSpec((None, 8, tn), lambda l, j: (l, 0, j)),
        out_shape=jax.ShapeDtypeStruct((depth, 8, d3), F32),
        compiler_params=_cparams(2),
    )(cc, ada_w, ada_b.reshape(depth, 1, d3))


def _lower_bound_kernel(x_ref, o_ref):
    depth = x_ref.shape[1]
    for d in range(2):
        x = x_ref[d]
        e = jnp.exp(x - jnp.max(x, axis=0, keepdims=True))
        soft = e / jnp.sum(e, axis=0, keepdims=True)
        run = jnp.zeros((1, x.shape[1]), F32)
        o_ref[d, 0:1, :] = run
        for l in range(1, depth):
            run = run + soft[l:l + 1, :]
            o_ref[d, l:l + 1, :] = run


def _lower_bounds(hgrn_lower_bounds):
    return pl.pallas_call(
        _lower_bound_kernel,
        out_shape=jax.ShapeDtypeStruct(hgrn_lower_bounds.shape, F32),
    )(hgrn_lower_bounds.astype(F32))


def _inproj_kernel(x_ref, mod_ref, nw_ref, cos_ref, sa_ref, sb_ref, w_ref, o_ref, h_ref,
                   *, n_lat, tm, tn, sub):
    i = pl.program_id(0)
    j = pl.program_id(1)
    d = x_ref.shape[1]

    @pl.when(j == 0)
    def _():
        nw = nw_ref[...]
        w_lat = nw * (1.0 + mod_ref[0:1, d:2 * d])
        w_ctx = nw * (1.0 + mod_ref[1:2, d:2 * d])

        def body(r, carry):
            rows = pl.ds(pl.multiple_of(r * sub, sub), sub)
            is_ctx = i * tm + r * sub >= n_lat
            w_eff = jnp.where(is_ctx, w_ctx, w_lat)
            shift = jnp.where(is_ctx, mod_ref[1:2, 0:d], mod_ref[0:1, 0:d])
            x = x_ref[rows, :]
            ms = jnp.mean(x * x, axis=-1, keepdims=True)
            h_ref[rows, :] = (x * lax.rsqrt(ms + NORM_EPS) * w_eff + shift).astype(BF16)
            return carry

        lax.fori_loop(0, tm // sub, body, 0, unroll=4)

    is_rope = j * tn == OFF_CQ
    halves = [slice(k * D_C, (k + 1) * D_C) for k in range(tn // D_C)]

    @pl.when(jnp.logical_not(is_rope))
    def _():
        for cs in halves:
            o_ref[:, cs] = jnp.dot(h_ref[...], w_ref[:, cs], preferred_element_type=F32).astype(o_ref.dtype)

    @pl.when(is_rope)
    def _():
        for k, cs in enumerate(halves):
            acc = jnp.dot(h_ref[...], w_ref[:, cs], preferred_element_type=F32)
            mult = Q_SCALE if k == 0 else 1.0
            c = cos_ref[...] * mult
            sa = sa_ref[...] * mult
            sb = sb_ref[...] * mult
            for g in range(D_C // LANES):
                xg = acc[:, g * LANES:(g + 1) * LANES]
                yg = xg * c + pltpu.roll(xg, 32, 1) * sa + pltpu.roll(xg, LANES - 32, 1) * sb
                o_ref[:, k * D_C + g * LANES:k * D_C + (g + 1) * LANES] = yg.astype(o_ref.dtype)


def _inproj(xa, mods, norm_w, cos_t, sa_t, sb_t, w_in_bf, *, layer, n_lat, tm, tn):
    rows, d = xa.shape
    d_in = w_in_bf.shape[2]
    sub = 32
    assert tn == OFF_CV - OFF_CQ == 2 * D_C and OFF_CQ % tn == 0 and n_lat % sub == 0 and tm % (4 * sub) == 0
    kern = functools.partial(_inproj_kernel, n_lat=n_lat, tm=tm, tn=tn, sub=sub)
    return pl.pallas_call(
        kern,
        grid=(rows // tm, d_in // tn),
        in_specs=[
            pl.BlockSpec((tm, d), lambda i, j: (i, 0)),
            pl.BlockSpec((None,) + mods.shape[1:], lambda i, j: (layer, 0, 0)),
            pl.BlockSpec((None, 1, d), lambda i, j: (layer, 0, 0)),
            pl.BlockSpec((tm, LANES), lambda i, j: (i, 0)),
            pl.BlockSpec((tm, LANES), lambda i, j: (i, 0)),
            pl.BlockSpec((tm, LANES), lambda i, j: (i, 0)),
            pl.BlockSpec((None, d, tn), lambda i, j: (layer, 0, j)),
        ],
        out_specs=pl.BlockSpec((tm, tn), lambda i, j: (i, j)),
        out_shape=jax.ShapeDtypeStruct((rows, d_in), BF16),
        scratch_shapes=[pltpu.VMEM((tm, d), BF16)],
        compiler_params=_cparams(2),
    )(xa, mods, norm_w, cos_t, sa_t, sb_t, w_in_bf)


def _attn_kernel(lam_ref, sw_ref, q_ref, k_ref, v_ref, z_ref,
                 o_ref, m_ref, l_ref, acc_ref, sa_ref, sb_ref, mpa_ref, mpb_ref, p_ref, al_ref,
                 *, chunks, lam_init, rb):
    tq = q_ref.shape[0]
    m_ref[...] = jnp.full(m_ref.shape, -jnp.inf, F32)
    l_ref[...] = jnp.zeros(l_ref.shape, F32)
    acc_ref[...] = jnp.zeros(acc_ref.shape, F32)
    cw = 2 * LANES

    def scores(row0, size, s_ref, mp_ref):
        for u in range(2):
            q = q_ref[:, u * C_HEAD_DIM:(u + 1) * C_HEAD_DIM]
            mp = None
            for cb in range(size // cw):
                k = k_ref[pl.ds(row0 + cb * cw, cw), u * C_HEAD_DIM:(u + 1) * C_HEAD_DIM]
                sc = _nt_dot(q, k)
                s_ref[u, :, cb * cw:(cb + 1) * cw] = sc
                part = jnp.maximum(sc[:, 0:LANES], sc[:, LANES:cw])
                mp = part if mp is None else jnp.maximum(mp, part)
            mp_ref[u] = mp

    def softmax_pv(row0, size, s_ref, mp_ref):
        v = v_ref[pl.ds(row0, size), :]
        for u in range(2):
            for r0 in range(0, tq, rb):
                r = slice(r0, r0 + rb)
                m_prev = m_ref[u, r, :]
                m_new = jnp.maximum(m_prev, jnp.max(mp_ref[u, r, :], axis=-1, keepdims=True))
                alpha = jnp.exp2(m_prev - m_new)
                p = jnp.exp2(s_ref[u, r, 0:size] - m_new)
                lp = p[:, 0:LANES]
                for g in range(1, size // LANES):
                    lp = lp + p[:, g * LANES:(g + 1) * LANES]
                l_ref[u, r, :] = alpha * l_ref[u, r, :] + lp
                p_ref[u, r, 0:size] = p.astype(BF16)
                al_ref[u, r, :] = alpha
                m_ref[u, r, :] = m_new
            acc_ref[u] = al_ref[u] * acc_ref[u] + jnp.dot(p_ref[u, :, 0:size], v, preferred_element_type=F32)

    slots = ((sa_ref, mpa_ref), (sb_ref, mpb_ref))
    row0 = 0
    starts = []
    for size in chunks:
        starts.append(row0)
        row0 += size
    scores(starts[0], chunks[0], *slots[0])
    for c, size in enumerate(chunks):
        if c + 1 < len(chunks):
            scores(starts[c + 1], chunks[c + 1], *slots[(c + 1) % 2])
        softmax_pv(starts[c], size, *slots[c % 2])

    lp = lam_ref[...]
    lam = (jnp.exp(jnp.sum(lp[0:1] * lp[1:2], axis=-1, keepdims=True))
           - jnp.exp(jnp.sum(lp[2:3] * lp[3:4], axis=-1, keepdims=True)) + lam_init)
    l0 = jnp.sum(l_ref[0], axis=-1, keepdims=True)
    l1 = jnp.sum(l_ref[1], axis=-1, keepdims=True)
    o = acc_ref[0] / l0 - lam * (acc_ref[1] / l1)
    ms = jnp.mean(o * o, axis=-1, keepdims=True)
    y = o * lax.rsqrt(ms + NORM_EPS) * sw_ref[...] * (1.0 - lam_init)
    o_ref[...] = (y * _silu(z_ref[...].astype(F32))).astype(o_ref.dtype)


def _key_chunks(n_kv, tk):
    unit = 2 * LANES
    assert n_kv % unit == 0
    sizes = [tk] * (n_kv // tk)
    if n_kv % tk:
        sizes.append(n_kv % tk)
    return tuple(sizes)


def _attention(p, lam_p, subln_w, *, layer, q_row0, n_q, kv_row0, n_kv, tq, tk, lam_init):
    qb0, kb0 = q_row0 // tq, kv_row0 // n_kv
    chunks = _key_chunks(n_kv, tk)
    ts = max(chunks)
    cq, ck, cv, cz = (OFF_CQ // C_VAL_DIM, OFF_CK // C_VAL_DIM, OFF_CV // C_VAL_DIM, OFF_CZ // C_VAL_DIM)
    kern = functools.partial(_attn_kernel, chunks=chunks, lam_init=lam_init, rb=ATTN_ROW_BLOCK)
    in_specs = [
        pl.BlockSpec((None, 4, C_HEAD_DIM), lambda h, i: (layer, 0, 0)),
        pl.BlockSpec((None, 1, C_VAL_DIM), lambda h, i: (layer, 0, 0)),
        pl.BlockSpec((tq, C_VAL_DIM), lambda h, i: (qb0 + i, cq + h)),
        pl.BlockSpec((n_kv, C_VAL_DIM), lambda h, i: (kb0, ck + h)),
        pl.BlockSpec((n_kv, C_VAL_DIM), lambda h, i: (kb0, cv + h)),
        pl.BlockSpec((tq, C_VAL_DIM), lambda h, i: (qb0 + i, cz + h)),
    ]
    return pl.pallas_call(
        kern,
        grid=(C_HEADS, n_q // tq),
        in_specs=in_specs,
        out_specs=pl.BlockSpec((tq, C_VAL_DIM), lambda h, i: (i, h)),
        out_shape=jax.ShapeDtypeStruct((n_q, D_C), BF16),
        scratch_shapes=[
            pltpu.VMEM((2, tq, 1), F32),
            pltpu.VMEM((2, tq, LANES), F32),
            pltpu.VMEM((2, tq, C_VAL_DIM), F32),
            pltpu.VMEM((2, tq, ts), F32),
            pltpu.VMEM((2, tq, ts), F32),
            pltpu.VMEM((2, tq, LANES), F32),
            pltpu.VMEM((2, tq, LANES), F32),
            pltpu.VMEM((2, tq, ts), BF16),
            pltpu.VMEM((2, tq, 1), F32),
        ],
        compiler_params=_cparams(2),
    )(lam_p, subln_w, p, p, p, p)


def _hgrn_constants():
    c = HGRN_CHUNK
    t = np.arange(c)
    mats = [(t[None, :] <= t[:, None]), (t[None, :] > t[:, None])]
    masks, rts = [], []
    for lvl in range(HGRN_LEVELS):
        n = (c // 2) >> lvl
        later = (t & n) != 0
        parent = t // (2 * n)
        mid = parent * 2 * n + n - 1
        dm = np.zeros((c, c), bool)
        for tt in range(c):
            if later[tt]:
                dm[tt, mid[tt] + 1:tt + 1] = True
            else:
                dm[tt, tt + 1:mid[tt] + 1] = True
        mats.append(dm)
        masks.append(later[:, None] & (~later[None, :]) & (parent[:, None] == parent[None, :]))
        rts.append(later)
    masks.append(np.eye(c, dtype=bool))
    cm = np.concatenate(mats, axis=0).astype(np.float32)
    mk = np.stack(masks).astype(np.float32)
    rt = np.repeat(np.stack(rts).astype(np.float32)[:, :, None], LANES, axis=2)

    def flip(a):
        blocks = a.reshape(-1, c, a.shape[-1])
        if a.shape[-1] == c:
            blocks = blocks[:, ::-1, ::-1]
        else:
            blocks = blocks[:, ::-1, :]
        return blocks.reshape(a.shape)

    cm2 = np.stack([cm, flip(cm)])
    cm2 = np.concatenate([cm2, cm2, cm2], axis=2)
    mk2 =np.stack([mk, flip(mk.reshape(-1, c)).reshape(mk.shape)])
    rt2 = np.stack([rt, flip(rt.reshape(-1, LANES)).reshape(rt.shape)])
    return cm2, mk2, rt2


def _hgrn_kernel(lbf_ref, lbb_ref, cm_ref, mk_ref, rt_ref,
                 qf_ref, vf_ref, af_ref, qb_ref, vb_ref, ab_ref,
                 of_ref, ob_ref, st_ref, f_ref, qs_ref, kk_ref, sc_ref, *, n_sub):
    c = HGRN_CHUNK

    @pl.when(pl.program_id(0) == 0)
    def _():
        st_ref[...] = jnp.zeros(st_ref.shape, F32)

    dirs = ((0, af_ref, qf_ref, vf_ref, lbf_ref, of_ref), (1, ab_ref, qb_ref, vb_ref, lbb_ref, ob_ref))

    def chunk_rows(d, t):
        blk = t if d == 0 else n_sub - 1 - t
        return slice(blk * c, (blk + 1) * c)

    for d, a_ref, q_ref, v_ref, lb_ref, o_ref in dirs:
        a = a_ref[...].astype(F32)
        lb = lb_ref[...]
        log_lb = jnp.log(lb)
        log_1m = jnp.log1p(-lb)
        t = jnp.exp(-jnp.abs(a))
        one_t = 1.0 + t
        log_sig = jnp.minimum(a, 0.0) - jnp.log(one_t)
        q2 = log_1m + log_sig
        g = jnp.maximum(log_lb, q2) + jnp.log(1.0 + jnp.exp(-jnp.abs(log_lb - q2)))
        g = g * LOG2_E
        r = 1.0 / one_t
        kk_ref[d] = (1.0 - lb) * jnp.where(a >= 0.0, t * r, r)
        qs_ref[d] = _silu(q_ref[...].astype(F32))
        g1 = g.astype(BF16)
        r1 = g - g1.astype(F32)
        g2 = r1.astype(BF16)
        g3 = (r1 - g2.astype(F32)).astype(BF16)
        for t in range(n_sub):
            r = chunk_rows(d, t)
            gcat = jnp.concatenate([g1[r], g2[r], g3[r]], axis=0)
            f_ref[d, t] = jnp.exp2(jnp.dot(cm_ref[d], gcat, preferred_element_type=F32))

    for d, a_ref, q_ref, v_ref, lb_ref, o_ref in dirs:
        q_side = [rt_ref[d, lvl] != 0.0 for lvl in range(HGRN_LEVELS)]
        for t in range(n_sub):
            r = chunk_rows(d, t)
            for h in range(B_HEADS):
                ln = slice(h * B_DIM, (h + 1) * B_DIM)
                qs = qs_ref[d, r, ln]
                kk = kk_ref[d, r, ln]
                sc = mk_ref[d, HGRN_LEVELS] * _nt_dot(qs.astype(BF16), kk.astype(BF16))
                for lvl in range(HGRN_LEVELS):
                    fl = f_ref[d, t, (2 + lvl) * c:(3 + lvl) * c, ln]
                    xl = (jnp.where(q_side[lvl], qs, kk) * fl).astype(BF16)
                    sc = sc + mk_ref[d, lvl] * _nt_dot(xl, xl)
                sc_ref[d, t, h] = sc.astype(BF16)

    for t in range(n_sub):
        for d, a_ref, q_ref, v_ref, lb_ref, o_ref in dirs:
            r = chunk_rows(d, t)
            last = c - 1 if d == 0 else 0
            for h in range(B_HEADS):
                ln = slice(h * B_DIM, (h + 1) * B_DIM)
                v32 = v_ref[r, ln].astype(F32)
                qe = (qs_ref[d, r, ln] * f_ref[d, t, 0:c, ln]).astype(BF16)
                kdec = (kk_ref[d, r, ln] * f_ref[d, t, c:2 * c, ln]).astype(BF16)
                e_last = f_ref[d, t, last:last + 1, ln]
                st = st_ref[d, h]
                o = (jnp.dot(sc_ref[d, t, h], v32.astype(BF16), preferred_element_type=F32)
                     + _nt_dot(qe, st.astype(BF16)))
                st_ref[d, h] = st * e_last + jnp.dot(v32.T.astype(BF16), kdec, preferred_element_type=F32)
                o_ref[r, ln] = o.astype(o_ref.dtype)


def _hgrn(p, lb_all, consts, *, layer, n_lat, n_ctx, tr):
    rows = p.shape[0]
    cm, mk, rt = consts
    n_lat_b, n_ctx_b = n_lat // tr, n_ctx // tr
    nb = n_lat_b + n_ctx_b
    n_sub = tr // HGRN_CHUNK

    def fwd_blk(s):
        return jnp.where(s < n_ctx_b, n_lat_b + s, s - n_ctx_b)

    def bwd_blk(s):
        return nb - 1 - s

    cq, ci, cff, cfb = OFF_BQ // D_B, OFF_BI // D_B, OFF_BFF // D_B, OFF_BFB // D_B
    full = lambda a: pl.BlockSpec(a.shape, lambda s: (0,) * a.ndim)
    kern = functools.partial(_hgrn_kernel, n_sub=n_sub)
    return pl.pallas_call(
        kern,
        grid=(nb,),
        in_specs=[
            pl.BlockSpec((None, None, 1, D_B), lambda s: (0, layer, 0, 0)),
            pl.BlockSpec((None, None, 1, D_B), lambda s: (1, layer, 0, 0)),
            full(cm), full(mk), full(rt),
            pl.BlockSpec((tr, D_B), lambda s: (fwd_blk(s), cq)),
            pl.BlockSpec((tr, D_B), lambda s: (fwd_blk(s), ci)),
            pl.BlockSpec((tr, D_B), lambda s: (fwd_blk(s), cff)),
            pl.BlockSpec((tr, D_B), lambda s: (bwd_blk(s), cq)),
            pl.BlockSpec((tr, D_B), lambda s: (bwd_blk(s), ci)),
            pl.BlockSpec((tr, D_B), lambda s: (bwd_blk(s), cfb)),
        ],
        out_specs=[
            pl.BlockSpec((tr, D_B), lambda s: (fwd_blk(s), 0)),
            pl.BlockSpec((tr, D_B), lambda s: (bwd_blk(s), 0)),
        ],
        out_shape=[jax.ShapeDtypeStruct((rows, D_B), F32)] * 2,
        scratch_shapes=[
            pltpu.VMEM((2, B_HEADS, B_DIM, B_DIM), F32),
            pltpu.VMEM((2, n_sub, (2 + HGRN_LEVELS) * HGRN_CHUNK, D_B), F32),
            pltpu.VMEM((2, tr, D_B), F32),
            pltpu.VMEM((2, tr, D_B), F32),
            pltpu.VMEM((2, n_sub, B_HEADS, HGRN_CHUNK, HGRN_CHUNK), BF16),
        ],
        compiler_params=_cparams(1),
    )(lb_all, lb_all, cm, mk, rt, p, p, p, p, p, p)


def _gelu(v):
    return 0.5 * v * (1.0 + lax.erf(v * (2.0 ** -0.5)))


def _out_kernel(x_ref, mod_ref, u_ref, v_ref, z_ref, bz_ref, of_ref, ob_ref, ycl_ref, ycc_ref,
                lnw_ref, lnb_ref, ws_ref, bs_ref, hw_ref, wout_ref, fw_ref,
                o_ref, y_ref, *, n_lat, tm, final):
    i = pl.program_id(0)
    d = x_ref.shape[1]
    n_lat_tiles = n_lat // tm

    def mix_chunk(n):
        rows = pl.ds(n * A_CHUNK, A_CHUNK)
        u = _gelu(u_ref[rows, :].astype(F32))
        v = _gelu(v_ref[rows, :].astype(F32))
        mu = jnp.mean(v, axis=-1, keepdims=True)
        vc = v - mu
        var = jnp.mean(vc * vc, axis=-1, keepdims=True)
        vln = (vc * lax.rsqrt(var + NORM_EPS) * lnw_ref[...] + lnb_ref[...]).astype(BF16)
        gz = _silu(z_ref[rows, :].astype(F32))
        ob = of_ref[rows, :] + ob_ref[rows, :]
        gbz = _silu(bz_ref[rows, :].astype(F32))
        for h in range(A_HEADS):
            ln = slice(h * LANES, (h + 1) * LANES)
            s = jnp.dot(ws_ref[h], vln[:, ln], preferred_element_type=F32) + bs_ref[h]
            y_ref[rows, h * LANES:(h + 1) * LANES] = (u[:, ln] * s * gz[:, ln]).astype(BF16)
            oh = ob[:, ln]
            ms = jnp.mean(oh * oh, axis=-1, keepdims=True)
            yb = oh * lax.rsqrt(ms + NORM_EPS) * hw_ref[...] * gbz[:, ln]
            y_ref[rows, D_A + h * LANES:D_A + (h + 1) * LANES] = yb.astype(BF16)

    yc = jnp.where(i < n_lat_tiles, ycl_ref[...], ycc_ref[...])
    out = jnp.dot(yc, wout_ref[D_A + D_B:, :], preferred_element_type=F32)
    for n in range(tm // A_CHUNK):
        mix_chunk(n)
    out = out + jnp.dot(y_ref[...], wout_ref[0:D_A + D_B, :], preferred_element_type=F32)
    gate =jnp.where(i >= n_lat_tiles, mod_ref[1:2, 2 * d:3 * d], mod_ref[0:1, 2 * d:3 * d])
    xn = x_ref[...] + gate * out
    if final:
        ms = jnp.mean(xn * xn, axis=-1, keepdims=True)
        xn = xn * lax.rsqrt(ms + NORM_EPS) * fw_ref[...]
    o_ref[...] = xn


def _outproj(xa, mods, p, o_f, o_b, yc_lat, yc_ctx, ln_w, ln_b, ws_bf, bs_b, hnorm_w, w_out_bf, final_w,
             *, layer, n_lat, n_rows, tm, final):
    d = xa.shape[1]
    assert n_lat % tm == 0 and n_rows % tm == 0 and yc_ctx.shape[0] % tm == 0
    n_lat_tiles = n_lat // tm
    kern = functools.partial(_out_kernel, n_lat=n_lat, tm=tm, final=final)
    cu, cv, cz, cbz = OFF_AU // D_A, OFF_AV // D_A, OFF_AZ // D_A, OFF_BZ // D_B
    full = lambda a: pl.BlockSpec(a.shape, lambda i: (0,) * a.ndim)
    layered = lambda a: pl.BlockSpec((None,) + a.shape[1:], lambda i: (layer,) + (0,) * (a.ndim - 1))
    return pl.pallas_call(
        kern,
        grid=(n_rows // tm,),
        in_specs=[
            pl.BlockSpec((tm, d), lambda i: (i, 0)),
            layered(mods),
            pl.BlockSpec((tm, D_A), lambda i: (i, cu)),
            pl.BlockSpec((tm, D_A), lambda i: (i, cv)),
            pl.BlockSpec((tm, D_A), lambda i: (i, cz)),
            pl.BlockSpec((tm, D_B), lambda i: (i, cbz)),
            pl.BlockSpec((tm, D_B), lambda i: (i, 0)),
            pl.BlockSpec((tm, D_B), lambda i: (i, 0)),
            pl.BlockSpec((tm, D_C), lambda i: (jnp.minimum(i, n_lat_tiles - 1), 0)),
            pl.BlockSpec((tm, D_C), lambda i: (jnp.maximum(i - n_lat_tiles, 0), 0)),
            layered(ln_w), layered(ln_b), layered(ws_bf), layered(bs_b), layered(hnorm_w), layered(w_out_bf),
            full(final_w),
        ],
        out_specs=pl.BlockSpec((tm, d), lambda i: (i, 0)),
        out_shape=jax.ShapeDtypeStruct((n_rows, d), F32),
        scratch_shapes=[pltpu.VMEM((tm, D_A + D_B), BF16)],
        compiler_params=_cparams(1),
    )(xa, mods, p, p, p, p, o_f, o_b, yc_lat, yc_ctx, ln_w, ln_b, ws_bf, bs_b, hnorm_w, w_out_bf, final_w)


def _rope_tables(n_lat, n_ctx):
    rows = n_lat // GRID_W
    row_ids = jnp.repeat(jnp.arange(rows, dtype=F32), GRID_W)
    col_ids = jnp.tile(jnp.arange(GRID_W, dtype=F32), rows)
    inv = ROPE_THETA ** (-jnp.arange(0, ROPE_AXIS_DIM, 2, dtype=F32) / ROPE_AXIS_DIM)
    ang_r = row_ids[:, None] * inv[None, :]
    ang_c = col_ids[:, None] * inv[None, :]
    zero = jnp.zeros_like(ang_r)
    cos_t = jnp.concatenate([jnp.cos(ang_r), jnp.cos(ang_r), jnp.cos(ang_c), jnp.cos(ang_c)], axis=1)
    sa_t = jnp.concatenate([zero, jnp.sin(ang_r), zero, jnp.sin(ang_c)], axis=1)
    sb_t = jnp.concatenate([-jnp.sin(ang_r), zero, -jnp.sin(ang_c), zero], axis=1)
    pad = lambda a, v: jnp.concatenate([a, jnp.full((n_ctx, LANES), v, F32)], axis=0)
    return pad(cos_t, 1.0), pad(sa_t, 0.0), pad(sb_t, 0.0)


def _pick_tile(n, candidates):
    for t in candidates:
        if n % t == 0:
            return t
    raise ValueError(f"no tile for {n}")


def kernel(x, c, ctx, c_ctx, ada_w, ada_b, norm_w, w_in, gmlp_ln_w, gmlp_ln_b, gmlp_ws, gmlp_bs,
           hgrn_lower_bounds, hgrn_norm_w, diff_lambda, diff_subln_w, w_out, final_norm_w):
    bsz, n_lat, d = x.shape
    n_ctx = ctx.shape[1]
    depth = ada_w.shape[0]
    assert bsz == 1 and d == D_MODEL
    assert n_lat % 512 == 0 and n_ctx % 256 == 0 and n_lat % GRID_W == 0
    n_rows = n_lat + n_ctx

    xa = jnp.concatenate([x[0], ctx[0]], axis=0)
    cc = jnp.zeros((8, d), F32).at[0].set(c[0]).at[1].set(c_ctx)
    mods = _adaln(cc, ada_w, ada_b)
    lb_all = _lower_bounds(hgrn_lower_bounds).reshape(2, depth, 1, D_B)
    cos_t, sa_t, sb_t = _rope_tables(n_lat, n_ctx)
    cm, mk, rt = _hgrn_constants()
    consts = (jnp.asarray(cm, BF16), jnp.asarray(mk), jnp.asarray(rt))

    w_in_bf = w_in.astype(BF16)
    w_out_bf = w_out.astype(BF16)
    ws_bf = gmlp_ws.astype(BF16)
    bs_b = jnp.broadcast_to(gmlp_bs[:, :, :, None], gmlp_bs.shape + (LANES,))
    norm_w3 = norm_w.reshape(depth, 1, d)
    ln_w3 = gmlp_ln_w.reshape(depth, 1, D_A)
    ln_b3 = gmlp_ln_b.reshape(depth, 1, D_A)
    hnorm_w3 = hgrn_norm_w.reshape(depth, 1, B_DIM)
    subln_w3 = diff_subln_w.reshape(depth, 1, C_VAL_DIM)
    final_w2 = final_norm_w.reshape(1, d)

    tm_in = _pick_tile(n_rows, (768, 256))
    tm_out = _pick_tile(n_ctx, (256,))
    tm_fin = _pick_tile(n_lat, (512,))
    tq = _pick_tile(n_lat, (512,))
    tk = 2048 if n_lat >= 8192 else 512

    for layer in range(depth):
        lam_init = 0.8 - 0.6 * math.exp(-0.3 * layer)
        last = layer == depth - 1
        p = _inproj(xa, mods, norm_w3, cos_t, sa_t, sb_t, w_in_bf,
                    layer=layer, n_lat=n_lat, tm=tm_in, tn=2 * D_C)
        yc_lat = _attention(p, diff_lambda, subln_w3, layer=layer, q_row0=0, n_q=n_lat,
                            kv_row0=0, n_kv=n_rows, tq=tq, tk=tk, lam_init=lam_init)
        if last:
            yc_ctx = yc_lat
        else:
            yc_ctx = _attention(p, diff_lambda, subln_w3, layer=layer, q_row0=n_lat, n_q=n_ctx,
                                kv_row0=n_lat, n_kv=n_ctx, tq=n_ctx, tk=n_ctx, lam_init=lam_init)
        o_f, o_b = _hgrn(p, lb_all, consts, layer=layer, n_lat=n_lat, n_ctx=n_ctx, tr=256)
        xa = _outproj(xa, mods, p, o_f, o_b, yc_lat, yc_ctx, ln_w3, ln_b3, ws_bf, bs_b, hnorm_w3,
                      w_out_bf, final_w2, layer=layer, n_lat=n_lat,
                      n_rows=n_lat if last else n_rows, tm=tm_fin if last else tm_out, final=last)
    return xa[None]
```

```python
import functools
import math

import numpy as np
import jax
import jax.numpy as jnp
from jax import lax
from jax.experimental import pallas as pl
from jax.experimental.pallas import tpu as pltpu

F32 = jnp.float32
BF16 = jnp.bfloat16

D_MODEL = 2048
GRID_W = 64
NORM_EPS = 1e-6
D_A = 512
A_HEADS = 4
A_CHUNK = 128
D_B = 512
B_HEADS = 4
B_DIM = 128
D_C = 1024
C_HEADS = 4
C_HEAD_DIM = 128
C_VAL_DIM = 256
ROPE_THETA = 10000.0
ROPE_AXIS_DIM = 64
D_IN = 3 * D_A + 5 * D_B + 4 * D_C
OFF_AU, OFF_AV, OFF_AZ = 0, 512, 1024
OFF_BQ, OFF_BI, OFF_BFF, OFF_BFB, OFF_BZ = 1536, 2048, 2560, 3072, 3584
OFF_CQ, OFF_CK, OFF_CV, OFF_CZ = 4096, 5120, 6144, 7168
Q_SCALE = C_HEAD_DIM ** -0.5 * math.log2(math.e)

LANES = 128
HGRN_CHUNK = 64
HGRN_LEVELS = 6
LOG2_E = math.log2(math.e)
ATTN_ROW_BLOCK = 32
VMEM_LIMIT = 56 * 1024 * 1024


def _cparams(n_axes):
    return pltpu.CompilerParams(
        dimension_semantics=("arbitrary",) * n_axes,
        vmem_limit_bytes=VMEM_LIMIT,
    )


def _silu(v):
    return v * jax.nn.sigmoid(v)


def _nt_dot(a, b):
    return lax.dot_general(a, b, (((1,), (1,)), ((), ())), preferred_element_type=F32)


def _adaln_kernel(c_ref, w_ref, b_ref, o_ref):
    d, tn = w_ref.shape
    ng = tn // LANES

    def body(k, acc):
        rows = pl.ds(pl.multiple_of(k * 8, 8), 8)
        w = w_ref[rows, :]
        out = []
        for r in range(2):
            s = _silu(c_ref[r, rows, :])
            out.append(tuple(acc[r][g] + s * w[:, g * LANES:(g + 1) * LANES] for g in range(ng)))
        return tuple(out)

    zero = tuple(tuple(jnp.zeros((8, LANES), F32) for _ in range(ng)) for _ in range(2))
    acc = lax.fori_loop(0, d // 8, body, zero, unroll=8)
    for r in range(2):
        row = jnp.concatenate([jnp.sum(a, axis=0, keepdims=True) for a in acc[r]], axis=1)
        o_ref[r:r + 1, :] = row + b_ref[...]
    o_ref[2:8, :] = jnp.zeros((6, tn), F32)


def _adaln(c_rep, ada_w, ada_b):
    depth, d, d3 = ada_w.shape
    tn = 1024
    return pl.pallas_call(
        _adaln_kernel,
        grid=(depth, d3 // tn),
        in_specs=[
            pl.BlockSpec((2, d, LANES), lambda l, j: (0, 0, 0)),
            pl.BlockSpec((None, d, tn), lambda l, j: (l, 0, j)),
            pl.BlockSpec((None, 1, tn), lambda l, j: (l, 0, j)),
        ],
        out_specs=pl.BlockSpec((None, 8, tn), lambda l, j: (l, 0, j)),
        out_shape=jax.ShapeDtypeStruct((depth, 8, d3), F32),
        compiler_params=_cparams(2),
    )(c_rep, ada_w, ada_b.reshape(depth, 1, d3))


def _lower_bound_kernel(x_ref, o_ref):
    depth = x_ref.shape[1]
    for d in range(2):
        x = x_ref[d]
        e = jnp.exp(x - jnp.max(x, axis=0, keepdims=True))
        soft = e / jnp.sum(e, axis=0, keepdims=True)
        run = jnp.zeros((1, x.shape[1]), F32)
        o_ref[d, 0:1, :] = run
        for l in range(1, depth):
            run = run + soft[l:l + 1, :]
            o_ref[d, l:l + 1, :] = run


def _lower_bounds(hgrn_lower_bounds):
    return pl.pallas_call(
        _lower_bound_kernel,
        out_shape=jax.ShapeDtypeStruct(hgrn_lower_bounds.shape, F32),
    )(hgrn_lower_bounds.astype(F32))


def _inproj_kernel(x_ref, mod_ref, nw_ref, cos_ref, sa_ref, sb_ref, w_ref, o_ref, h_ref,
                   *, n_lat, tm, tn, sub):
    i = pl.program_id(0)
    j = pl.program_id(1)
    d = x_ref.shape[1]

    @pl.when(j == 0)
    def _():
        nw = nw_ref[...]
        w_lat = nw * (1.0 + mod_ref[0:1, d:2 * d])
        w_ctx = nw * (1.0 + mod_ref[1:2, d:2 * d])

        def body(r, carry):
            rows = pl.ds(pl.multiple_of(r * sub, sub), sub)
            is_ctx = i * tm + r * sub >= n_lat
            w_eff = jnp.where(is_ctx, w_ctx, w_lat)
            shift = jnp.where(is_ctx, mod_ref[1:2, 0:d], mod_ref[0:1, 0:d])
            x = x_ref[rows, :]
            ms = jnp.mean(x * x, axis=-1, keepdims=True)
            h_ref[rows, :] = (x * lax.rsqrt(ms + NORM_EPS) * w_eff + shift).astype(BF16)
            return carry

        lax.fori_loop(0, tm // sub, body, 0, unroll=4)

    is_rope = j * tn == OFF_CQ
    halves = [slice(k * D_C, (k + 1) * D_C) for k in range(tn // D_C)]

    @pl.when(jnp.logical_not(is_rope))
    def _():
        for cs in halves:
            o_ref[:, cs] = jnp.dot(h_ref[...], w_ref[:, cs], preferred_element_type=F32).astype(o_ref.dtype)

    @pl.when(is_rope)
    def _():
        for k, cs in enumerate(halves):
            acc = jnp.dot(h_ref[...], w_ref[:, cs], preferred_element_type=F32)
            mult = Q_SCALE if k == 0 else 1.0
            c = cos_ref[...] * mult
            sa = sa_ref[...] * mult
            sb = sb_ref[...] * mult
            for g in range(D_C // LANES):
                xg = acc[:, g * LANES:(g + 1) * LANES]
                yg = xg * c + pltpu.roll(xg, 32, 1) * sa + pltpu.roll(xg, LANES - 32, 1) * sb
                o_ref[:, k * D_C + g * LANES:k * D_C + (g + 1) * LANES] = yg.astype(o_ref.dtype)


def _inproj(xa, mods, norm_w, cos_t, sa_t, sb_t, w_in_bf, *, layer, n_lat, tm, tn):
    rows, d = xa.shape
    d_in = w_in_bf.shape[2]
    sub = 32
    assert tn == OFF_CV - OFF_CQ == 2 * D_C and OFF_CQ % tn == 0 and n_lat % sub == 0 and tm % (4 * sub) == 0
    kern = functools.partial(_inproj_kernel, n_lat=n_lat, tm=tm, tn=tn, sub=sub)
    return pl.pallas_call(
        kern,
        grid=(rows // tm, d_in // tn),
        in_specs=[
            pl.BlockSpec((tm, d), lambda i, j: (i, 0)),
            pl.BlockSpec((None,) + mods.shape[1:], lambda i, j: (layer, 0, 0)),
            pl.BlockSpec((None, 1, d), lambda i, j: (layer, 0, 0)),
            pl.BlockSpec((tm, LANES), lambda i, j: (i, 0)),
            pl.BlockSpec((tm, LANES), lambda i, j: (i, 0)),
            pl.BlockSpec((tm, LANES), lambda i, j: (i, 0)),
            pl.BlockSpec((None, d, tn), lambda i, j: (layer, 0, j)),
        ],
        out_specs=pl.BlockSpec((tm, tn), lambda i, j: (i, j)),
        out_shape=jax.ShapeDtypeStruct((rows, d_in), BF16),
        scratch_shapes=[pltpu.VMEM((tm, d), BF16)],
        compiler_params=_cparams(2),
    )(xa, mods, norm_w, cos_t, sa_t, sb_t, w_in_bf)


def _attn_kernel(lam_ref, sw_ref, q_ref, k_ref, v_ref, z_ref,
                 o_ref, m_ref, l_ref, acc_ref, sa_ref, sb_ref, mpa_ref, mpb_ref, p_ref, al_ref,
                 *, chunks, lam_init, rb):
    tq = q_ref.shape[0]
    m_ref[...] = jnp.full(m_ref.shape, -jnp.inf, F32)
    l_ref[...] = jnp.zeros(l_ref.shape, F32)
    acc_ref[...] = jnp.zeros(acc_ref.shape, F32)
    cw = 2 * LANES

    def scores(row0, size, s_ref, mp_ref):
        for u in range(2):
            q = q_ref[:, u * C_HEAD_DIM:(u + 1) * C_HEAD_DIM]
            mp = None
            for cb in range(size // cw):
                k = k_ref[pl.ds(row0 + cb * cw, cw), u * C_HEAD_DIM:(u + 1) * C_HEAD_DIM]
                sc = _nt_dot(q, k)
                s_ref[u, :, cb * cw:(cb + 1) * cw] = sc
                part = jnp.maximum(sc[:, 0:LANES], sc[:, LANES:cw])
                mp = part if mp is None else jnp.maximum(mp, part)
            mp_ref[u] = mp

    def softmax_pv(row0, size, s_ref, mp_ref):
        v = v_ref[pl.ds(row0, size), :]
        for u in range(2):
            for r0 in range(0, tq, rb):
                r = slice(r0, r0 + rb)
                m_prev = m_ref[u, r, :]
                m_new = jnp.maximum(m_prev, jnp.max(mp_ref[u, r, :], axis=-1, keepdims=True))
                alpha = jnp.exp2(m_prev - m_new)
                p = jnp.exp2(s_ref[u, r, 0:size] - m_new)
                lp = p[:, 0:LANES]
                for g in range(1, size // LANES):
                    lp = lp + p[:, g * LANES:(g + 1) * LANES]
                l_ref[u, r, :] = alpha * l_ref[u, r, :] + lp
                p_ref[u, r, 0:size] = p.astype(BF16)
                al_ref[u, r, :] = alpha
                m_ref[u, r, :] = m_new
            acc_ref[u] = al_ref[u] * acc_ref[u] + jnp.dot(p_ref[u, :, 0:size], v, preferred_element_type=F32)

    slots = ((sa_ref, mpa_ref), (sb_ref, mpb_ref))
    row0 = 0
    starts = []
    for size in chunks:
        starts.append(row0)
        row0 += size
    scores(starts[0], chunks[0], *slots[0])
    for c, size in enumerate(chunks):
        if c + 1 < len(chunks):
            scores(starts[c + 1], chunks[c + 1], *slots[(c + 1) % 2])
        softmax_pv(starts[c], size, *slots[c % 2])

    lp = lam_ref[...]
    lam = (jnp.exp(jnp.sum(lp[0:1] * lp[1:2], axis=-1, keepdims=True))
           - jnp.exp(jnp.sum(lp[2:3] * lp[3:4], axis=-1, keepdims=True)) + lam_init)
    l0 = jnp.sum(l_ref[0], axis=-1, keepdims=True)
    l1 = jnp.sum(l_ref[1], axis=-1, keepdims=True)
    o = acc_ref[0] / l0 - lam * (acc_ref[1] / l1)
    ms = jnp.mean(o * o, axis=-1, keepdims=True)
    y = o * lax.rsqrt(ms + NORM_EPS) * sw_ref[...] * (1.0 - lam_init)
    o_ref[...] = (y * _silu(z_ref[...].astype(F32))).astype(o_ref.dtype)


def _key_chunks(n_kv, tk):
    unit = 2 * LANES
    assert n_kv % unit == 0
    sizes = [tk] * (n_kv // tk)
    if n_kv % tk:
        sizes.append(n_kv % tk)
    return tuple(sizes)


def _attention(p, lam_p, subln_w, *, layer, q_row0, n_q, kv_row0, n_kv, tq, tk, lam_init):
    qb0, kb0 = q_row0 // tq, kv_row0 // n_kv
    chunks = _key_chunks(n_kv, tk)
    ts = max(chunks)
    cq, ck, cv, cz = (OFF_CQ // C_VAL_DIM, OFF_CK // C_VAL_DIM, OFF_CV // C_VAL_DIM, OFF_CZ // C_VAL_DIM)
    kern = functools.partial(_attn_kernel, chunks=chunks, lam_init=lam_init, rb=ATTN_ROW_BLOCK)
    in_specs = [
        pl.BlockSpec((None, 4, C_HEAD_DIM), lambda h, i: (layer, 0, 0)),
        pl.BlockSpec((None, 1, C_VAL_DIM), lambda h, i: (layer, 0, 0)),
        pl.BlockSpec((tq, C_VAL_DIM), lambda h, i: (qb0 + i, cq + h)),
        pl.BlockSpec((n_kv, C_VAL_DIM), lambda h, i: (kb0, ck + h)),
        pl.BlockSpec((n_kv, C_VAL_DIM), lambda h, i: (kb0, cv + h)),
        pl.BlockSpec((tq, C_VAL_DIM), lambda h, i: (qb0 + i, cz + h)),
    ]
    return pl.pallas_call(
        kern,
        grid=(C_HEADS, n_q // tq),
        in_specs=in_specs,
        out_specs=pl.BlockSpec((tq, C_VAL_DIM), lambda h, i: (i, h)),
        out_shape=jax.ShapeDtypeStruct((n_q, D_C), BF16),
        scratch_shapes=[
            pltpu.VMEM((2, tq, 1), F32),
            pltpu.VMEM((2, tq, LANES), F32),
            pltpu.VMEM((2, tq, C_VAL_DIM), F32),
            pltpu.VMEM((2, tq, ts), F32),
            pltpu.VMEM((2, tq, ts), F32),
            pltpu.VMEM((2, tq, LANES), F32),
            pltpu.VMEM((2, tq, LANES), F32),
            pltpu.VMEM((2, tq, ts), BF16),
            pltpu.VMEM((2, tq, 1), F32),
        ],
        compiler_params=_cparams(2),
    )(lam_p, subln_w, p, p, p, p)


def _hgrn_constants():
    c = HGRN_CHUNK
    t = np.arange(c)
    mats = [(t[None, :] <= t[:, None]), (t[None, :] > t[:, None])]
    masks, rts = [], []
    for lvl in range(HGRN_LEVELS):
        n = (c // 2) >> lvl
        later = (t & n) != 0
        parent = t // (2 * n)
        mid = parent * 2 * n + n - 1
        dm = np.zeros((c, c), bool)
        for tt in range(c):
            if later[tt]:
                dm[tt, mid[tt] + 1:tt + 1] = True
            else:
                dm[tt, tt + 1:mid[tt] + 1] = True
        mats.append(dm)
        masks.append(later[:, None] & (~later[None, :]) & (parent[:, None] == parent[None, :]))
        rts.append(later)
    masks.append(np.eye(c, dtype=bool))
    cm = np.concatenate(mats, axis=0).astype(np.float32)
    mk = np.stack(masks).astype(np.float32)
    rt = np.repeat(np.stack(rts).astype(np.float32)[:, :, None], LANES, axis=2)

    def flip(a):
        blocks = a.reshape(-1, c, a.shape[-1])
        if a.shape[-1] == c:
            blocks = blocks[:, ::-1, ::-1]
        else:
            blocks = blocks[:, ::-1, :]
        return blocks.reshape(a.shape)

    cm2 = np.stack([cm, flip(cm)])
    cm2 = np.concatenate([cm2, cm2, cm2], axis=2)
    mk2 =np.stack([mk, flip(mk.reshape(-1, c)).reshape(mk.shape)])
    rt2 = np.stack([rt, flip(rt.reshape(-1, LANES)).reshape(rt.shape)])
    return cm2, mk2, rt2


def _hgrn_kernel(lbf_ref, lbb_ref, cm_ref, mk_ref, rt_ref,
                 qf_ref, vf_ref, af_ref, qb_ref, vb_ref, ab_ref,
                 of_ref, ob_ref, st_ref, f_ref, qs_ref, kk_ref, sc_ref, *, n_sub):
    c = HGRN_CHUNK

    @pl.when(pl.program_id(0) == 0)
    def _():
        st_ref[...] = jnp.zeros(st_ref.shape, F32)

    dirs = ((0, af_ref, qf_ref, vf_ref, lbf_ref, of_ref), (1, ab_ref, qb_ref, vb_ref, lbb_ref, ob_ref))

    def chunk_rows(d, t):
        blk = t if d == 0 else n_sub - 1 - t
        return slice(blk * c, (blk + 1) * c)

    for d, a_ref, q_ref, v_ref, lb_ref, o_ref in dirs:
        a = a_ref[...].astype(F32)
        lb = lb_ref[...]
        log_lb = jnp.log(lb)
        log_1m = jnp.log1p(-lb)
        log_sig = jnp.minimum(a, 0.0) - jnp.log(1.0 + jnp.exp(-jnp.abs(a)))
        q2 = log_1m + log_sig
        g = jnp.maximum(log_lb, q2) + jnp.log(1.0 + jnp.exp(-jnp.abs(log_lb - q2)))
        g = g * LOG2_E
        kk_ref[d] = (1.0 - lb) * jax.nn.sigmoid(-a)
        qs_ref[d] = _silu(q_ref[...].astype(F32))
        g1 = g.astype(BF16)
        r1 = g - g1.astype(F32)
        g2 = r1.astype(BF16)
        g3 = (r1 - g2.astype(F32)).astype(BF16)
        for t in range(n_sub):
            r = chunk_rows(d, t)
            gcat = jnp.concatenate([g1[r], g2[r], g3[r]], axis=0)
            f_ref[d, t] = jnp.exp2(jnp.dot(cm_ref[d], gcat, preferred_element_type=F32))

    for d, a_ref, q_ref, v_ref, lb_ref, o_ref in dirs:
        q_side = [rt_ref[d, lvl] != 0.0 for lvl in range(HGRN_LEVELS)]
        for t in range(n_sub):
            r = chunk_rows(d, t)
            for h in range(B_HEADS):
                ln = slice(h * B_DIM, (h + 1) * B_DIM)
                qs = qs_ref[d, r, ln]
                kk = kk_ref[d, r, ln]
                sc = mk_ref[d, HGRN_LEVELS] * _nt_dot(qs.astype(BF16), kk.astype(BF16))
                for lvl in range(HGRN_LEVELS):
                    fl = f_ref[d, t, (2 + lvl) * c:(3 + lvl) * c, ln]
                    xl = (jnp.where(q_side[lvl], qs, kk) * fl).astype(BF16)
                    sc = sc + mk_ref[d, lvl] * _nt_dot(xl, xl)
                sc_ref[d, t, h] = sc.astype(BF16)

    for t in range(n_sub):
        for d, a_ref, q_ref, v_ref, lb_ref, o_ref in dirs:
            r = chunk_rows(d, t)
            last = c - 1 if d == 0 else 0
            for h in range(B_HEADS):
                ln = slice(h * B_DIM, (h + 1) * B_DIM)
                v32 = v_ref[r, ln].astype(F32)
                qe = (qs_ref[d, r, ln] * f_ref[d, t, 0:c, ln]).astype(BF16)
                kdec = (kk_ref[d, r, ln] * f_ref[d, t, c:2 * c, ln]).astype(BF16)
                e_last = f_ref[d, t, last:last + 1, ln]
                st = st_ref[d, h]
                o = (jnp.dot(sc_ref[d, t, h], v32.astype(BF16), preferred_element_type=F32)
                     + _nt_dot(qe, st.astype(BF16)))
                st_ref[d, h] = st * e_last + jnp.dot(v32.T.astype(BF16), kdec, preferred_element_type=F32)
                o_ref[r, ln] = o.astype(o_ref.dtype)


def _hgrn(p, lb_all, consts, *, layer, n_lat, n_ctx, tr):
    rows = p.shape[0]
    cm, mk, rt = consts
    n_lat_b, n_ctx_b = n_lat // tr, n_ctx // tr
    nb = n_lat_b + n_ctx_b
    n_sub = tr // HGRN_CHUNK

    def fwd_blk(s):
        return jnp.where(s < n_ctx_b, n_lat_b + s, s - n_ctx_b)

    def bwd_blk(s):
        return nb - 1 - s

    cq, ci, cff, cfb = OFF_BQ // D_B, OFF_BI // D_B, OFF_BFF // D_B, OFF_BFB // D_B
    full = lambda a: pl.BlockSpec(a.shape, lambda s: (0,) * a.ndim)
    kern = functools.partial(_hgrn_kernel, n_sub=n_sub)
    return pl.pallas_call(
        kern,
        grid=(nb,),
        in_specs=[
            pl.BlockSpec((None, None, 1, D_B), lambda s: (0, layer, 0, 0)),
            pl.BlockSpec((None, None, 1, D_B), lambda s: (1, layer, 0, 0)),
            full(cm), full(mk), full(rt),
            pl.BlockSpec((tr, D_B), lambda s: (fwd_blk(s), cq)),
            pl.BlockSpec((tr, D_B), lambda s: (fwd_blk(s), ci)),
            pl.BlockSpec((tr, D_B), lambda s: (fwd_blk(s), cff)),
            pl.BlockSpec((tr, D_B), lambda s: (bwd_blk(s), cq)),
            pl.BlockSpec((tr, D_B), lambda s: (bwd_blk(s), ci)),
            pl.BlockSpec((tr, D_B), lambda s: (bwd_blk(s), cfb)),
        ],
        out_specs=[
            pl.BlockSpec((tr, D_B), lambda s: (fwd_blk(s), 0)),
            pl.BlockSpec((tr, D_B), lambda s: (bwd_blk(s), 0)),
        ],
        out_shape=[jax.ShapeDtypeStruct((rows, D_B), F32)] * 2,
        scratch_shapes=[
            pltpu.VMEM((2, B_HEADS, B_DIM, B_DIM), F32),
            pltpu.VMEM((2, n_sub, (2 + HGRN_LEVELS) * HGRN_CHUNK, D_B), F32),
            pltpu.VMEM((2, tr, D_B), F32),
            pltpu.VMEM((2, tr, D_B), F32),
            pltpu.VMEM((2, n_sub, B_HEADS, HGRN_CHUNK, HGRN_CHUNK), BF16),
        ],
        compiler_params=_cparams(1),
    )(lb_all, lb_all, cm, mk, rt, p, p, p, p, p, p)


def _gelu(v):
    return 0.5 * v * (1.0 + lax.erf(v * (2.0 ** -0.5)))


def _out_kernel(x_ref, mod_ref, u_ref, v_ref, z_ref, bz_ref, of_ref, ob_ref, ycl_ref, ycc_ref,
                lnw_ref, lnb_ref, ws_ref, bs_ref, hw_ref, wout_ref, fw_ref,
                o_ref, y_ref, *, n_lat, tm, final):
    i = pl.program_id(0)
    d = x_ref.shape[1]
    n_lat_tiles = n_lat // tm

    def mix_chunk(n):
        rows = pl.ds(n * A_CHUNK, A_CHUNK)
        u = _gelu(u_ref[rows, :].astype(F32))
        v = _gelu(v_ref[rows, :].astype(F32))
        mu = jnp.mean(v, axis=-1, keepdims=True)
        vc = v - mu
        var = jnp.mean(vc * vc, axis=-1, keepdims=True)
        vln = (vc * lax.rsqrt(var + NORM_EPS) * lnw_ref[...] + lnb_ref[...]).astype(BF16)
        gz = _silu(z_ref[rows, :].astype(F32))
        ob = of_ref[rows, :] + ob_ref[rows, :]
        gbz = _silu(bz_ref[rows, :].astype(F32))
        for h in range(A_HEADS):
            ln = slice(h * LANES, (h + 1) * LANES)
            s = jnp.dot(ws_ref[h], vln[:, ln], preferred_element_type=F32) + bs_ref[h]
            y_ref[rows, h * LANES:(h + 1) * LANES] = (u[:, ln] * s * gz[:, ln]).astype(BF16)
            oh = ob[:, ln]
            ms = jnp.mean(oh * oh, axis=-1, keepdims=True)
            yb = oh * lax.rsqrt(ms + NORM_EPS) * hw_ref[...] * gbz[:, ln]
            y_ref[rows, D_A + h * LANES:D_A + (h + 1) * LANES] = yb.astype(BF16)

    yc = jnp.where(i < n_lat_tiles, ycl_ref[...], ycc_ref[...])
    out = jnp.dot(yc, wout_ref[D_A + D_B:, :], preferred_element_type=F32)
    for n in range(tm // A_CHUNK):
        mix_chunk(n)
    out = out + jnp.dot(y_ref[...], wout_ref[0:D_A + D_B, :], preferred_element_type=F32)
    gate =jnp.where(i >= n_lat_tiles, mod_ref[1:2, 2 * d:3 * d], mod_ref[0:1, 2 * d:3 * d])
    xn = x_ref[...] + gate * out
    if final:
        ms = jnp.mean(xn * xn, axis=-1, keepdims=True)
        xn = xn * lax.rsqrt(ms + NORM_EPS) * fw_ref[...]
    o_ref[...] = xn


def _outproj(xa, mods, p, o_f, o_b, yc_lat, yc_ctx, ln_w, ln_b, ws_bf, bs_b, hnorm_w, w_out_bf, final_w,
             *, layer, n_lat, n_rows, tm, final):
    d = xa.shape[1]
    assert n_lat % tm == 0 and n_rows % tm == 0 and yc_ctx.shape[0] % tm == 0
    n_lat_tiles = n_lat // tm
    kern = functools.partial(_out_kernel, n_lat=n_lat, tm=tm, final=final)
    cu, cv, cz, cbz = OFF_AU // D_A, OFF_AV // D_A, OFF_AZ // D_A, OFF_BZ // D_B
    full = lambda a: pl.BlockSpec(a.shape, lambda i: (0,) * a.ndim)
    layered = lambda a: pl.BlockSpec((None,) + a.shape[1:], lambda i: (layer,) + (0,) * (a.ndim - 1))
    return pl.pallas_call(
        kern,
        grid=(n_rows // tm,),
        in_specs=[
            pl.BlockSpec((tm, d), lambda i: (i, 0)),
            layered(mods),
            pl.BlockSpec((tm, D_A), lambda i: (i, cu)),
            pl.BlockSpec((tm, D_A), lambda i: (i, cv)),
            pl.BlockSpec((tm, D_A), lambda i: (i, cz)),
            pl.BlockSpec((tm, D_B), lambda i: (i, cbz)),
            pl.BlockSpec((tm, D_B), lambda i: (i, 0)),
            pl.BlockSpec((tm, D_B), lambda i: (i, 0)),
            pl.BlockSpec((tm, D_C), lambda i: (jnp.minimum(i, n_lat_tiles - 1), 0)),
            pl.BlockSpec((tm, D_C), lambda i: (jnp.maximum(i - n_lat_tiles, 0), 0)),
            layered(ln_w), layered(ln_b), layered(ws_bf), layered(bs_b), layered(hnorm_w), layered(w_out_bf),
            full(final_w),
        ],
        out_specs=pl.BlockSpec((tm, d), lambda i: (i, 0)),
        out_shape=jax.ShapeDtypeStruct((n_rows, d), F32),
        scratch_shapes=[pltpu.VMEM((tm, D_A + D_B), BF16)],
        compiler_params=_cparams(1),
    )(xa, mods, p, p, p, p, o_f, o_b, yc_lat, yc_ctx, ln_w, ln_b, ws_bf, bs_b, hnorm_w, w_out_bf, final_w)


def _rope_tables(n_lat, n_ctx):
    rows = n_lat // GRID_W
    inv = (ROPE_THETA ** (-np.arange(0, ROPE_AXIS_DIM, 2, dtype=np.float32) / ROPE_AXIS_DIM)).astype(np.float32)
    ang_r = np.arange(rows, dtype=np.float32)[:, None] * inv[None, :]
    ang_c = np.arange(GRID_W, dtype=np.float32)[:, None] * inv[None, :]
    per_row = lambda a: jnp.repeat(jnp.asarray(a, F32), GRID_W, axis=0)
    per_col = lambda a: jnp.tile(jnp.asarray(a, F32), (rows, 1))
    cos_r, sin_r = per_row(np.cos(ang_r)), per_row(np.sin(ang_r))
    cos_c, sin_c = per_col(np.cos(ang_c)), per_col(np.sin(ang_c))
    zero = jnp.zeros_like(cos_r)
    cos_t = jnp.concatenate([cos_r, cos_r, cos_c, cos_c], axis=1)
    sa_t = jnp.concatenate([zero, sin_r, zero, sin_c], axis=1)
    sb_t = jnp.concatenate([-sin_r, zero, -sin_c, zero], axis=1)
    pad = lambda a, v: jnp.concatenate([a, jnp.full((n_ctx, LANES), v, F32)], axis=0)
    return pad(cos_t, 1.0), pad(sa_t, 0.0), pad(sb_t, 0.0)


def _pick_tile(n, candidates):
    for t in candidates:
        if n % t == 0:
            return t
    raise ValueError(f"no tile for {n}")


def kernel(x, c, ctx, c_ctx, ada_w, ada_b, norm_w, w_in, gmlp_ln_w, gmlp_ln_b, gmlp_ws, gmlp_bs,
           hgrn_lower_bounds, hgrn_norm_w, diff_lambda, diff_subln_w, w_out, final_norm_w):
    bsz, n_lat, d = x.shape
    n_ctx = ctx.shape[1]
    depth = ada_w.shape[0]
    assert bsz == 1 and d == D_MODEL
    assert n_lat % 512 == 0 and n_ctx % 256 == 0 and n_lat % GRID_W == 0
    n_rows = n_lat + n_ctx

    xa = jnp.concatenate([x[0], ctx[0]], axis=0)
    c_rep = jnp.broadcast_to(jnp.stack([c[0], c_ctx])[:, :, None], (2, d, LANES))
    mods = _adaln(c_rep, ada_w, ada_b)
    lb_all = _lower_bounds(hgrn_lower_bounds).reshape(2, depth, 1, D_B)
    cos_t, sa_t, sb_t = _rope_tables(n_lat, n_ctx)
    cm, mk, rt = _hgrn_constants()
    consts = (jnp.asarray(cm, BF16), jnp.asarray(mk), jnp.asarray(rt))

    w_in_bf = w_in.astype(BF16)
    w_out_bf = w_out.astype(BF16)
    ws_bf = gmlp_ws.astype(BF16)
    bs_b = jnp.broadcast_to(gmlp_bs[:, :, :, None], gmlp_bs.shape + (LANES,))
    norm_w3 = norm_w.reshape(depth, 1, d)
    ln_w3 = gmlp_ln_w.reshape(depth, 1, D_A)
    ln_b3 = gmlp_ln_b.reshape(depth, 1, D_A)
    hnorm_w3 = hgrn_norm_w.reshape(depth, 1, B_DIM)
    subln_w3 = diff_subln_w.reshape(depth, 1, C_VAL_DIM)
    final_w2 = final_norm_w.reshape(1, d)

    tm_in = _pick_tile(n_rows, (768, 256))
    tm_out = _pick_tile(n_ctx, (256,))
    tm_fin = _pick_tile(n_lat, (512,))
    tq = _pick_tile(n_lat, (512,))
    tk = 2048 if n_lat >= 8192 else 512

    for layer in range(depth):
        lam_init = 0.8 - 0.6 * math.exp(-0.3 * layer)
        last = layer == depth - 1
        p = _inproj(xa, mods, norm_w3, cos_t, sa_t, sb_t, w_in_bf,
                    layer=layer, n_lat=n_lat, tm=tm_in, tn=2 * D_C)
        yc_lat = _attention(p, diff_lambda, subln_w3, layer=layer, q_row0=0, n_q=n_lat,
                            kv_row0=0, n_kv=n_rows, tq=tq, tk=tk, lam_init=lam_init)
        if last:
            yc_ctx = yc_lat
        else:
            yc_ctx = _attention(p, diff_lambda, subln_w3, layer=layer, q_row0=n_lat, n_q=n_ctx,
                                kv_row0=n_lat, n_kv=n_ctx, tq=n_ctx, tk=n_ctx, lam_init=lam_init)
        o_f, o_b = _hgrn(p, lb_all, consts, layer=layer, n_lat=n_lat, n_ctx=n_ctx, tr=256)
        xa = _outproj(xa, mods, p, o_f, o_b, yc_lat, yc_ctx, ln_w3, ln_b3, ws_bf, bs_b, hnorm_w3,
                      w_out_bf, final_w2, layer=layer, n_lat=n_lat,
                      n_rows=n_lat if last else n_rows, tm=tm_fin if last else tm_out, final=last)
    return xa[None]
```

```python
import functools
import math

import numpy as np
import jax
import jax.numpy as jnp
from jax import lax
from jax.experimental import pallas as pl
from jax.experimental.pallas import tpu as pltpu

F32 = jnp.float32
BF16 = jnp.bfloat16

D_MODEL = 2048
GRID_W = 64
NORM_EPS = 1e-6
D_A = 512
A_HEADS = 4
A_CHUNK = 128
D_B = 512
B_HEADS = 4
B_DIM = 128
D_C = 1024
C_HEADS = 4
C_HEAD_DIM = 128
C_VAL_DIM = 256
ROPE_THETA = 10000.0
ROPE_AXIS_DIM = 64
D_IN = 3 * D_A + 5 * D_B + 4 * D_C
OFF_AU, OFF_AV, OFF_AZ = 0, 512, 1024
OFF_BQ, OFF_BI, OFF_BFF, OFF_BFB, OFF_BZ = 1536, 2048, 2560, 3072, 3584
OFF_CQ, OFF_CK, OFF_CV, OFF_CZ = 4096, 5120, 6144, 7168
Q_SCALE = C_HEAD_DIM ** -0.5 * math.log2(math.e)

LANES = 128
HGRN_CHUNK = 64
HGRN_LEVELS = 6
LOG2_E = math.log2(math.e)
ATTN_ROW_BLOCK = 32
VMEM_LIMIT = 56 * 1024 * 1024


def _cparams(n_axes):
    return pltpu.CompilerParams(
        dimension_semantics=("arbitrary",) * n_axes,
        vmem_limit_bytes=VMEM_LIMIT,
    )


def _silu(v):
    return v * jax.nn.sigmoid(v)


def _nt_dot(a, b):
    return lax.dot_general(a, b, (((1,), (1,)), ((), ())), preferred_element_type=F32)


def _adaln_kernel(c_ref, w_ref, b_ref, o_ref):
    d, tn = w_ref.shape
    ng = tn // LANES

    def body(k, acc):
        rows = pl.ds(pl.multiple_of(k * 8, 8), 8)
        w = w_ref[rows, :]
        out = []
        for r in range(2):
            s = _silu(c_ref[r, rows, :])
            out.append(tuple(acc[r][g] + s * w[:, g * LANES:(g + 1) * LANES] for g in range(ng)))
        return tuple(out)

    zero = tuple(tuple(jnp.zeros((8, LANES), F32) for _ in range(ng)) for _ in range(2))
    acc = lax.fori_loop(0, d // 8, body, zero, unroll=8)
    for r in range(2):
        row = jnp.concatenate([jnp.sum(a, axis=0, keepdims=True) for a in acc[r]], axis=1)
        o_ref[r:r + 1, :] = row + b_ref[...]
    o_ref[2:8, :] = jnp.zeros((6, tn), F32)


def _adaln(c_rep, ada_w, ada_b):
    depth, d, d3 = ada_w.shape
    tn = 1024
    return pl.pallas_call(
        _adaln_kernel,
        grid=(depth, d3 // tn),
        in_specs=[
            pl.BlockSpec((2, d, LANES), lambda l, j: (0, 0, 0)),
            pl.BlockSpec((None, d, tn), lambda l, j: (l, 0, j)),
            pl.BlockSpec((None, 1, tn), lambda l, j: (l, 0, j)),
        ],
        out_specs=pl.BlockSpec((None, 8, tn), lambda l, j: (l, 0, j)),
        out_shape=jax.ShapeDtypeStruct((depth, 8, d3), F32),
        compiler_params=_cparams(2),
    )(c_rep, ada_w, ada_b.reshape(depth, 1, d3))


def _lower_bound_kernel(x_ref, o_ref):
    depth = x_ref.shape[1]
    for d in range(2):
        x = x_ref[d]
        e = jnp.exp(x - jnp.max(x, axis=0, keepdims=True))
        soft = e / jnp.sum(e, axis=0, keepdims=True)
        run = jnp.zeros((1, x.shape[1]), F32)
        o_ref[d, 0:1, :] = run
        for l in range(1, depth):
            run = run + soft[l:l + 1, :]
            o_ref[d, l:l + 1, :] = run


def _lower_bounds(hgrn_lower_bounds):
    return pl.pallas_call(
        _lower_bound_kernel,
        out_shape=jax.ShapeDtypeStruct(hgrn_lower_bounds.shape, F32),
    )(hgrn_lower_bounds.astype(F32))


def _inproj_kernel(x_ref, c_ref, mod_ref, nw_ref, cos_ref, sa_ref, sb_ref, w_ref, o_ref, h_ref,
                   *, n_ctx, tm, tn, sub, n_tiles):
    i = pl.program_id(0)
    j = pl.program_id(1)
    n_col = pl.num_programs(1)
    d = x_ref.shape[1]
    slot = i % 2
    lat_rows_last = tm - n_ctx

    def normalize(dst, tile_is_last):
        nw = nw_ref[...]

        def rows_to_h(src_ref, dst0, n_sub, m):
            w_eff = nw * (1.0 + mod_ref[m:m + 1, d:2 * d])
            shift = mod_ref[m:m + 1, 0:d]
            for r in range(n_sub):
                x = src_ref[r * sub:(r + 1) * sub, :]
                ms = jnp.mean(x * x, axis=-1, keepdims=True)
                h_ref[dst, dst0 + r * sub:dst0 + (r + 1) * sub, :] = (
                    x * lax.rsqrt(ms + NORM_EPS) * w_eff + shift).astype(BF16)

        if tile_is_last:
            rows_to_h(x_ref, 0, lat_rows_last // sub, 0)
            rows_to_h(c_ref, lat_rows_last, n_ctx // sub, 1)
        else:
            rows_to_h(x_ref, 0, tm // sub, 0)

    @pl.when(jnp.logical_and(i == 0, j == 0))
    def _():
        normalize(0, n_tiles == 1)

    is_rope = j * tn == OFF_CQ
    last_col = j == n_col - 1
    halves = [slice(k * D_C, (k + 1) * D_C) for k in range(tn // D_C)]

    def plain_dots():
        h = h_ref[slot]
        for cs in halves:
            o_ref[:, cs] = jnp.dot(h, w_ref[:, cs], preferred_element_type=F32).astype(o_ref.dtype)

    @pl.when(jnp.logical_and(jnp.logical_not(is_rope), jnp.logical_not(last_col)))
    def _():
        plain_dots()

    @pl.when(jnp.logical_and(last_col, i + 1 < n_tiles - 1))
    def _():
        plain_dots()
        normalize(1 - slot, False)

    @pl.when(jnp.logical_and(last_col, i + 1 == n_tiles - 1))
    def _():
        plain_dots()
        normalize(1 - slot, True)

    @pl.when(jnp.logical_and(last_col, i + 1 == n_tiles))
    def _():
        plain_dots()

    @pl.when(is_rope)
    def _():
        h = h_ref[slot]
        for k, cs in enumerate(halves):
            acc = jnp.dot(h, w_ref[:, cs], preferred_element_type=F32)
            mult = Q_SCALE if k == 0 else 1.0
            c = cos_ref[...] * mult
            sa = sa_ref[...] * mult
            sb = sb_ref[...] * mult
            for g in range(D_C // LANES):
                xg = acc[:, g * LANES:(g + 1) * LANES]
                yg = xg * c + pltpu.roll(xg, 32, 1) * sa + pltpu.roll(xg, LANES - 32, 1) * sb
                o_ref[:, k * D_C + g * LANES:k * D_C + (g + 1) * LANES] = yg.astype(o_ref.dtype)


def _inproj(x_lat, x_ctx, ctx_blk, mods, norm_w, cos_t, sa_t, sb_t, w_in_bf, *, layer, n_lat, n_ctx, tm, tn):
    d = x_lat.shape[1]
    d_in = w_in_bf.shape[2]
    n_rows = n_lat + n_ctx
    n_tiles = n_rows // tm
    n_col = d_in // tn
    sub = 32
    assert tn == OFF_CV - OFF_CQ == 2 * D_C and OFF_CQ % tn == 0 and OFF_CQ // tn < n_col - 1
    assert n_rows % tm == 0 and n_ctx <= tm and n_ctx % sub == 0 and tm % sub == 0
    kern = functools.partial(_inproj_kernel, n_ctx=n_ctx, tm=tm, tn=tn, sub=sub, n_tiles=n_tiles)

    def x_tile(i, j):
        return (jnp.minimum(jnp.where(j == n_col - 1, i + 1, i), (n_lat - 1) // tm), 0)

    return pl.pallas_call(
        kern,
        grid=(n_tiles, n_col),
        in_specs=[
            pl.BlockSpec((tm, d), x_tile),
            pl.BlockSpec((n_ctx, d), lambda i, j: (ctx_blk, 0)),
            pl.BlockSpec((None,) + mods.shape[1:], lambda i, j: (layer, 0, 0)),
            pl.BlockSpec((None, 1, d), lambda i, j: (layer, 0, 0)),
            pl.BlockSpec((tm, LANES), lambda i, j: (i, 0)),
            pl.BlockSpec((tm, LANES), lambda i, j: (i, 0)),
            pl.BlockSpec((tm, LANES), lambda i, j: (i, 0)),
            pl.BlockSpec((None, d, tn), lambda i, j: (layer, 0, j)),
        ],
        out_specs=pl.BlockSpec((tm, tn), lambda i, j: (i, j)),
        out_shape=jax.ShapeDtypeStruct((n_rows, d_in), BF16),
        scratch_shapes=[pltpu.VMEM((2, tm, d), BF16)],
        compiler_params=_cparams(2),
    )(x_lat, x_ctx, mods, norm_w, cos_t, sa_t, sb_t, w_in_bf)


def _attn_kernel(lam_ref, sw_ref, q_ref, k_ref, v_ref, z_ref,
                 o_ref, m_ref, l_ref, acc_ref, sa_ref, sb_ref, mpa_ref, mpb_ref, p_ref, al_ref,
                 *, chunks, lam_init, rb):
    tq = q_ref.shape[0]
    m_ref[...] = jnp.full(m_ref.shape, -jnp.inf, F32)
    l_ref[...] = jnp.zeros(l_ref.shape, F32)
    acc_ref[...] = jnp.zeros(acc_ref.shape, F32)
    cw = 2 * LANES

    def scores(row0, size, s_ref, mp_ref):
        for u in range(2):
            q = q_ref[:, u * C_HEAD_DIM:(u + 1) * C_HEAD_DIM]
            mp = None
            for cb in range(size // cw):
                k = k_ref[pl.ds(row0 + cb * cw, cw), u * C_HEAD_DIM:(u + 1) * C_HEAD_DIM]
                sc = _nt_dot(q, k)
                s_ref[u, :, cb * cw:(cb + 1) * cw] = sc
                part = jnp.maximum(sc[:, 0:LANES], sc[:, LANES:cw])
                mp = part if mp is None else jnp.maximum(mp, part)
            mp_ref[u] = mp

    def softmax_pv(row0, size, s_ref, mp_ref):
        v = v_ref[pl.ds(row0, size), :]
        for u in range(2):
            for r0 in range(0, tq, rb):
                r = slice(r0, r0 + rb)
                m_prev = m_ref[u, r, :]
                m_new = jnp.maximum(m_prev, jnp.max(mp_ref[u, r, :], axis=-1, keepdims=True))
                alpha = jnp.exp2(m_prev - m_new)
                p = jnp.exp2(s_ref[u, r, 0:size] - m_new)
                lp = p[:, 0:LANES]
                for g in range(1, size // LANES):
                    lp = lp + p[:, g * LANES:(g + 1) * LANES]
                l_ref[u, r, :] = alpha * l_ref[u, r, :] + lp
                p_ref[u, r, 0:size] = p.astype(BF16)
                al_ref[u, r, :] = alpha
                m_ref[u, r, :] = m_new
            acc_ref[u] = al_ref[u] * acc_ref[u] + jnp.dot(p_ref[u, :, 0:size], v, preferred_element_type=F32)

    slots = ((sa_ref, mpa_ref), (sb_ref, mpb_ref))
    row0 = 0
    starts = []
    for size in chunks:
        starts.append(row0)
        row0 += size
    scores(starts[0], chunks[0], *slots[0])
    for c, size in enumerate(chunks):
        if c + 1 < len(chunks):
            scores(starts[c + 1], chunks[c + 1], *slots[(c + 1) % 2])
        softmax_pv(starts[c], size, *slots[c % 2])

    lp = lam_ref[...]
    lam = (jnp.exp(jnp.sum(lp[0:1] * lp[1:2], axis=-1, keepdims=True))
           - jnp.exp(jnp.sum(lp[2:3] * lp[3:4], axis=-1, keepdims=True)) + lam_init)
    l0 = jnp.sum(l_ref[0], axis=-1, keepdims=True)
    l1 = jnp.sum(l_ref[1], axis=-1, keepdims=True)
    o = acc_ref[0] / l0 - lam * (acc_ref[1] / l1)
    ms = jnp.mean(o * o, axis=-1, keepdims=True)
    y = o * lax.rsqrt(ms + NORM_EPS) * sw_ref[...] * (1.0 - lam_init)
    o_ref[...] = (y * _silu(z_ref[...].astype(F32))).astype(o_ref.dtype)


def _key_chunks(n_kv, tk):
    unit = 2 * LANES
    assert n_kv % unit == 0
    sizes = [tk] * (n_kv // tk)
    if n_kv % tk:
        sizes.append(n_kv % tk)
    return tuple(sizes)


def _attention(p, lam_p, subln_w, *, layer, q_row0, n_q, kv_row0, n_kv, tq, tk, lam_init):
    qb0, kb0 = q_row0 // tq, kv_row0 // n_kv
    chunks = _key_chunks(n_kv, tk)
    ts = max(chunks)
    cq, ck, cv, cz = (OFF_CQ // C_VAL_DIM, OFF_CK // C_VAL_DIM, OFF_CV // C_VAL_DIM, OFF_CZ // C_VAL_DIM)
    kern = functools.partial(_attn_kernel, chunks=chunks, lam_init=lam_init, rb=ATTN_ROW_BLOCK)
    in_specs = [
        pl.BlockSpec((None, 4, C_HEAD_DIM), lambda h, i: (layer, 0, 0)),
        pl.BlockSpec((None, 1, C_VAL_DIM), lambda h, i: (layer, 0, 0)),
        pl.BlockSpec((tq, C_VAL_DIM), lambda h, i: (qb0 + i, cq + h)),
        pl.BlockSpec((n_kv, C_VAL_DIM), lambda h, i: (kb0, ck + h)),
        pl.BlockSpec((n_kv, C_VAL_DIM), lambda h, i: (kb0, cv + h)),
        pl.BlockSpec((tq, C_VAL_DIM), lambda h, i: (qb0 + i, cz + h)),
    ]
    return pl.pallas_call(
        kern,
        grid=(C_HEADS, n_q // tq),
        in_specs=in_specs,
        out_specs=pl.BlockSpec((tq, C_VAL_DIM), lambda h, i: (i, h)),
        out_shape=jax.ShapeDtypeStruct((n_q, D_C), BF16),
        scratch_shapes=[
            pltpu.VMEM((2, tq, 1), F32),
            pltpu.VMEM((2, tq, LANES), F32),
            pltpu.VMEM((2, tq, C_VAL_DIM), F32),
            pltpu.VMEM((2, tq, ts), F32),
            pltpu.VMEM((2, tq, ts), F32),
            pltpu.VMEM((2, tq, LANES), F32),
            pltpu.VMEM((2, tq, LANES), F32),
            pltpu.VMEM((2, tq, ts), BF16),
            pltpu.VMEM((2, tq, 1), F32),
        ],
        compiler_params=_cparams(2),
    )(lam_p, subln_w, p, p, p, p)


def _hgrn_constants():
    c = HGRN_CHUNK
    t = np.arange(c)
    mats = [(t[None, :] <= t[:, None]), (t[None, :] > t[:, None])]
    masks, rts = [], []
    for lvl in range(HGRN_LEVELS):
        n = (c // 2) >> lvl
        later = (t & n) != 0
        parent = t // (2 * n)
        mid = parent * 2 * n + n - 1
        dm = np.zeros((c, c), bool)
        for tt in range(c):
            if later[tt]:
                dm[tt, mid[tt] + 1:tt + 1] = True
            else:
                dm[tt, tt + 1:mid[tt] + 1] = True
        mats.append(dm)
        masks.append(later[:, None] & (~later[None, :]) & (parent[:, None] == parent[None, :]))
        rts.append(later)
    masks.append(np.eye(c, dtype=bool))
    cm = np.concatenate(mats, axis=0).astype(np.float32)
    mk = np.stack(masks).astype(np.float32)
    rt = np.repeat(np.stack(rts).astype(np.float32)[:, :, None], LANES, axis=2)

    def flip(a):
        blocks = a.reshape(-1, c, a.shape[-1])
        if a.shape[-1] == c:
            blocks = blocks[:, ::-1, ::-1]
        else:
            blocks = blocks[:, ::-1, :]
        return blocks.reshape(a.shape)

    cm2 = np.stack([cm, flip(cm)])
    cm2 = np.concatenate([cm2, cm2, cm2], axis=2)
    mk2 =np.stack([mk, flip(mk.reshape(-1, c)).reshape(mk.shape)])
    rt2 = np.stack([rt, flip(rt.reshape(-1, LANES)).reshape(rt.shape)])
    return cm2, mk2, rt2


def _hgrn_kernel(lbf_ref, lbb_ref, cm_ref, mk_ref, rt_ref,
                 qf_ref, vf_ref, af_ref, qb_ref, vb_ref, ab_ref,
                 of_ref, ob_ref, st_ref, f_ref, qs_ref, kk_ref, sc_ref, *, n_sub):
    c = HGRN_CHUNK

    @pl.when(pl.program_id(0) == 0)
    def _():
        st_ref[...] = jnp.zeros(st_ref.shape, F32)

    dirs = ((0, af_ref, qf_ref, vf_ref, lbf_ref, of_ref), (1, ab_ref, qb_ref, vb_ref, lbb_ref, ob_ref))

    def chunk_rows(d, t):
        blk = t if d == 0 else n_sub - 1 - t
        return slice(blk * c, (blk + 1) * c)

    for d, a_ref, q_ref, v_ref, lb_ref, o_ref in dirs:
        a = a_ref[...].astype(F32)
        lb = lb_ref[...]
        log_lb = jnp.log(lb)
        log_1m = jnp.log1p(-lb)
        log_sig = jnp.minimum(a, 0.0) - jnp.log(1.0 + jnp.exp(-jnp.abs(a)))
        q2 = log_1m + log_sig
        g = jnp.maximum(log_lb, q2) + jnp.log(1.0 + jnp.exp(-jnp.abs(log_lb - q2)))
        g = g * LOG2_E
        kk_ref[d] = (1.0 - lb) * jax.nn.sigmoid(-a)
        qs_ref[d] = _silu(q_ref[...].astype(F32))
        g1 = g.astype(BF16)
        r1 = g - g1.astype(F32)
        g2 = r1.astype(BF16)
        g3 = (r1 - g2.astype(F32)).astype(BF16)
        for t in range(n_sub):
            r = chunk_rows(d, t)
            gcat = jnp.concatenate([g1[r], g2[r], g3[r]], axis=0)
            f_ref[d, t] = jnp.exp2(jnp.dot(cm_ref[d], gcat, preferred_element_type=F32))

    for d, a_ref, q_ref, v_ref, lb_ref, o_ref in dirs:
        q_side = [rt_ref[d, lvl] != 0.0 for lvl in range(HGRN_LEVELS)]
        for t in range(n_sub):
            r = chunk_rows(d, t)
            for h in range(B_HEADS):
                ln = slice(h * B_DIM, (h + 1) * B_DIM)
                qs = qs_ref[d, r, ln]
                kk = kk_ref[d, r, ln]
                sc = mk_ref[d, HGRN_LEVELS] * _nt_dot(qs.astype(BF16), kk.astype(BF16))
                for lvl in range(HGRN_LEVELS):
                    fl = f_ref[d, t, (2 + lvl) * c:(3 + lvl) * c, ln]
                    xl = (jnp.where(q_side[lvl], qs, kk) * fl).astype(BF16)
                    sc = sc + mk_ref[d, lvl] * _nt_dot(xl, xl)
                sc_ref[d, t, h] = sc.astype(BF16)

    for t in range(n_sub):
        for d, a_ref, q_ref, v_ref, lb_ref, o_ref in dirs:
            r = chunk_rows(d, t)
            last = c - 1 if d == 0 else 0
            for h in range(B_HEADS):
                ln = slice(h * B_DIM, (h + 1) * B_DIM)
                v32 = v_ref[r, ln].astype(F32)
                qe = (qs_ref[d, r, ln] * f_ref[d, t, 0:c, ln]).astype(BF16)
                kdec = (kk_ref[d, r, ln] * f_ref[d, t, c:2 * c, ln]).astype(BF16)
                e_last = f_ref[d, t, last:last + 1, ln]
                st = st_ref[d, h]
                o = (jnp.dot(sc_ref[d, t, h], v32.astype(BF16), preferred_element_type=F32)
                     + _nt_dot(qe, st.astype(BF16)))
                st_ref[d, h] = st * e_last + jnp.dot(v32.T.astype(BF16), kdec, preferred_element_type=F32)
                o_ref[r, ln] = o.astype(o_ref.dtype)


def _hgrn(p, lb_all, consts, *, layer, n_lat, n_ctx, tr):
    rows = p.shape[0]
    cm, mk, rt = consts
    n_lat_b, n_ctx_b = n_lat // tr, n_ctx // tr
    nb = n_lat_b + n_ctx_b
    n_sub = tr // HGRN_CHUNK

    def fwd_blk(s):
        return jnp.where(s < n_ctx_b, n_lat_b + s, s - n_ctx_b)

    def bwd_blk(s):
        return nb - 1 - s

    cq, ci, cff, cfb = OFF_BQ // D_B, OFF_BI // D_B, OFF_BFF // D_B, OFF_BFB // D_B
    full = lambda a: pl.BlockSpec(a.shape, lambda s: (0,) * a.ndim)
    kern = functools.partial(_hgrn_kernel, n_sub=n_sub)
    return pl.pallas_call(
        kern,
        grid=(nb,),
        in_specs=[
            pl.BlockSpec((None, None, 1, D_B), lambda s: (0, layer, 0, 0)),
            pl.BlockSpec((None, None, 1, D_B), lambda s: (1, layer, 0, 0)),
            full(cm), full(mk), full(rt),
            pl.BlockSpec((tr, D_B), lambda s: (fwd_blk(s), cq)),
            pl.BlockSpec((tr, D_B), lambda s: (fwd_blk(s), ci)),
            pl.BlockSpec((tr, D_B), lambda s: (fwd_blk(s), cff)),
            pl.BlockSpec((tr, D_B), lambda s: (bwd_blk(s), cq)),
            pl.BlockSpec((tr, D_B), lambda s: (bwd_blk(s), ci)),
            pl.BlockSpec((tr, D_B), lambda s: (bwd_blk(s), cfb)),
        ],
        out_specs=[
            pl.BlockSpec((tr, D_B), lambda s: (fwd_blk(s), 0)),
            pl.BlockSpec((tr, D_B), lambda s: (bwd_blk(s), 0)),
        ],
        out_shape=[jax.ShapeDtypeStruct((rows, D_B), F32)] * 2,
        scratch_shapes=[
            pltpu.VMEM((2, B_HEADS, B_DIM, B_DIM), F32),
            pltpu.VMEM((2, n_sub, (2 + HGRN_LEVELS) * HGRN_CHUNK, D_B), F32),
            pltpu.VMEM((2, tr, D_B), F32),
            pltpu.VMEM((2, tr, D_B), F32),
            pltpu.VMEM((2, n_sub, B_HEADS, HGRN_CHUNK, HGRN_CHUNK), BF16),
        ],
        compiler_params=_cparams(1),
    )(lb_all, lb_all, cm, mk, rt, p, p, p, p, p, p)


def _gelu(v):
    return 0.5 * v * (1.0 + lax.erf(v * (2.0 ** -0.5)))


def _out_kernel(x_ref, xc_ref, mod_ref, u_ref, v_ref, z_ref, bz_ref, of_ref, ob_ref, ycl_ref, ycc_ref,
                lnw_ref, lnb_ref, ws_ref, bs_ref, hw_ref, wout_ref, fw_ref,
                o_ref, y_ref, *, n_lat, tm, final):
    i = pl.program_id(0)
    d = x_ref.shape[1]
    n_lat_tiles = n_lat // tm

    def mix_chunk(n):
        rows = pl.ds(n * A_CHUNK, A_CHUNK)
        u = _gelu(u_ref[rows, :].astype(F32))
        v = _gelu(v_ref[rows, :].astype(F32))
        mu = jnp.mean(v, axis=-1, keepdims=True)
        vc = v - mu
        var = jnp.mean(vc * vc, axis=-1, keepdims=True)
        vln = (vc * lax.rsqrt(var + NORM_EPS) * lnw_ref[...] + lnb_ref[...]).astype(BF16)
        gz = _silu(z_ref[rows, :].astype(F32))
        ob = of_ref[rows, :] + ob_ref[rows, :]
        gbz = _silu(bz_ref[rows, :].astype(F32))
        for h in range(A_HEADS):
            ln = slice(h * LANES, (h + 1) * LANES)
            s = jnp.dot(ws_ref[h], vln[:, ln], preferred_element_type=F32) + bs_ref[h]
            y_ref[rows, h * LANES:(h + 1) * LANES] = (u[:, ln] * s * gz[:, ln]).astype(BF16)
            oh = ob[:, ln]
            ms = jnp.mean(oh * oh, axis=-1, keepdims=True)
            yb = oh * lax.rsqrt(ms + NORM_EPS) * hw_ref[...] * gbz[:, ln]
            y_ref[rows, D_A + h * LANES:D_A + (h + 1) * LANES] = yb.astype(BF16)

    yc = jnp.where(i < n_lat_tiles, ycl_ref[...], ycc_ref[...])
    out = jnp.dot(yc, wout_ref[D_A + D_B:, :], preferred_element_type=F32)
    for n in range(tm // A_CHUNK):
        mix_chunk(n)
    out = out + jnp.dot(y_ref[...], wout_ref[0:D_A + D_B, :], preferred_element_type=F32)
    gate = jnp.where(i >= n_lat_tiles, mod_ref[1:2, 2 * d:3 * d], mod_ref[0:1, 2 * d:3 * d])
    xn = jnp.where(i < n_lat_tiles, x_ref[...], xc_ref[...]) + gate * out
    if final:
        ms = jnp.mean(xn * xn, axis=-1, keepdims=True)
        xn = xn * lax.rsqrt(ms + NORM_EPS) * fw_ref[...]
    o_ref[...] = xn


def _outproj(x_lat, x_ctx, ctx_row0, mods, p, o_f, o_b, yc_lat, yc_ctx, ln_w, ln_b, ws_bf, bs_b, hnorm_w,
             w_out_bf, final_w, *, layer, n_lat, n_rows, tm, final):
    d = x_lat.shape[1]
    assert ctx_row0 % tm == 0
    ctx_blk0 = ctx_row0 // tm
    assert n_lat % tm == 0 and n_rows % tm == 0 and yc_ctx.shape[0] % tm == 0
    n_lat_tiles = n_lat // tm
    kern = functools.partial(_out_kernel, n_lat=n_lat, tm=tm, final=final)
    cu, cv, cz, cbz = OFF_AU // D_A, OFF_AV // D_A, OFF_AZ // D_A, OFF_BZ // D_B
    full = lambda a: pl.BlockSpec(a.shape, lambda i: (0,) * a.ndim)
    layered = lambda a: pl.BlockSpec((None,) + a.shape[1:], lambda i: (layer,) + (0,) * (a.ndim - 1))
    return pl.pallas_call(
        kern,
        grid=(n_rows // tm,),
        in_specs=[
            pl.BlockSpec((tm, d), lambda i: (jnp.minimum(i, n_lat_tiles - 1), 0)),
            pl.BlockSpec((tm, d), lambda i: (ctx_blk0 + jnp.maximum(i - n_lat_tiles, 0), 0)),
            layered(mods),
            pl.BlockSpec((tm, D_A), lambda i: (i, cu)),
            pl.BlockSpec((tm, D_A), lambda i: (i, cv)),
            pl.BlockSpec((tm, D_A), lambda i: (i, cz)),
            pl.BlockSpec((tm, D_B), lambda i: (i, cbz)),
            pl.BlockSpec((tm, D_B), lambda i: (i, 0)),
            pl.BlockSpec((tm, D_B), lambda i: (i, 0)),
            pl.BlockSpec((tm, D_C), lambda i: (jnp.minimum(i, n_lat_tiles - 1), 0)),
            pl.BlockSpec((tm, D_C), lambda i: (jnp.maximum(i - n_lat_tiles, 0), 0)),
            layered(ln_w), layered(ln_b), layered(ws_bf), layered(bs_b), layered(hnorm_w), layered(w_out_bf),
            full(final_w),
        ],
        out_specs=pl.BlockSpec((tm, d), lambda i: (i, 0)),
        out_shape=jax.ShapeDtypeStruct((n_rows, d), F32),
        scratch_shapes=[pltpu.VMEM((tm, D_A + D_B), BF16)],
        compiler_params=_cparams(1),
    )(x_lat, x_ctx, mods, p, p, p, p, o_f, o_b, yc_lat, yc_ctx, ln_w, ln_b, ws_bf, bs_b, hnorm_w, w_out_bf, final_w)


def _rope_tables(n_lat, n_ctx):
    rows = n_lat // GRID_W
    inv = (ROPE_THETA ** (-np.arange(0, ROPE_AXIS_DIM, 2, dtype=np.float32) / ROPE_AXIS_DIM)).astype(np.float32)
    ang_r = np.arange(rows, dtype=np.float32)[:, None] * inv[None, :]
    ang_c = np.arange(GRID_W, dtype=np.float32)[:, None] * inv[None, :]
    per_row = lambda a: jnp.repeat(jnp.asarray(a, F32), GRID_W, axis=0)
    per_col = lambda a: jnp.tile(jnp.asarray(a, F32), (rows, 1))
    cos_r, sin_r = per_row(np.cos(ang_r)), per_row(np.sin(ang_r))
    cos_c, sin_c = per_col(np.cos(ang_c)), per_col(np.sin(ang_c))
    zero = jnp.zeros_like(cos_r)
    cos_t = jnp.concatenate([cos_r, cos_r, cos_c, cos_c], axis=1)
    sa_t = jnp.concatenate([zero, sin_r, zero, sin_c], axis=1)
    sb_t = jnp.concatenate([-sin_r, zero, -sin_c, zero], axis=1)
    pad = lambda a, v: jnp.concatenate([a, jnp.full((n_ctx, LANES), v, F32)], axis=0)
    return pad(cos_t, 1.0), pad(sa_t, 0.0), pad(sb_t, 0.0)


def _pick_tile(n, candidates):
    for t in candidates:
        if n % t == 0:
            return t
    raise ValueError(f"no tile for {n}")


def kernel(x, c, ctx, c_ctx, ada_w, ada_b, norm_w, w_in, gmlp_ln_w, gmlp_ln_b, gmlp_ws, gmlp_bs,
           hgrn_lower_bounds, hgrn_norm_w, diff_lambda, diff_subln_w, w_out, final_norm_w):
    bsz, n_lat, d = x.shape
    n_ctx = ctx.shape[1]
    depth = ada_w.shape[0]
    assert bsz == 1 and d == D_MODEL
    assert n_lat % 512 == 0 and n_ctx % 256 == 0 and n_lat % GRID_W == 0
    n_rows = n_lat + n_ctx

    x_lat, x_ctx, ctx_row0 = x[0], ctx[0], 0
    c_rep = jnp.broadcast_to(jnp.stack([c[0], c_ctx])[:, :, None], (2, d, LANES))
    mods = _adaln(c_rep, ada_w, ada_b)
    lb_all = _lower_bounds(hgrn_lower_bounds).reshape(2, depth, 1, D_B)
    cos_t, sa_t, sb_t = _rope_tables(n_lat, n_ctx)
    cm, mk, rt = _hgrn_constants()
    consts = (jnp.asarray(cm, BF16), jnp.asarray(mk), jnp.asarray(rt))

    w_in_bf = w_in.astype(BF16)
    w_out_bf = w_out.astype(BF16)
    ws_bf = gmlp_ws.astype(BF16)
    bs_b = jnp.broadcast_to(gmlp_bs[:, :, :, None], gmlp_bs.shape + (LANES,))
    norm_w3 = norm_w.reshape(depth, 1, d)
    ln_w3 = gmlp_ln_w.reshape(depth, 1, D_A)
    ln_b3 = gmlp_ln_b.reshape(depth, 1, D_A)
    hnorm_w3 = hgrn_norm_w.reshape(depth, 1, B_DIM)
    subln_w3 = diff_subln_w.reshape(depth, 1, C_VAL_DIM)
    final_w2 = final_norm_w.reshape(1, d)

    tm_in = _pick_tile(n_rows, (768, 256))
    tm_out = _pick_tile(n_ctx, (256,))
    tm_fin = _pick_tile(n_lat, (512,))
    tq = _pick_tile(n_lat, (512,))
    tk = 2048 if n_lat >= 8192 else 512

    for layer in range(depth):
        lam_init = 0.8 - 0.6 * math.exp(-0.3 * layer)
        last = layer == depth - 1
        p = _inproj(x_lat, x_ctx, ctx_row0 // n_ctx, mods, norm_w3, cos_t, sa_t, sb_t, w_in_bf,
                    layer=layer, n_lat=n_lat, n_ctx=n_ctx, tm=tm_in, tn=2 * D_C)
        yc_lat = _attention(p, diff_lambda, subln_w3, layer=layer, q_row0=0, n_q=n_lat,
                            kv_row0=0, n_kv=n_rows, tq=tq, tk=tk, lam_init=lam_init)
        if last:
            yc_ctx = yc_lat
        else:
            yc_ctx = _attention(p, diff_lambda, subln_w3, layer=layer, q_row0=n_lat, n_q=n_ctx,
                                kv_row0=n_lat, n_kv=n_ctx, tq=n_ctx, tk=n_ctx, lam_init=lam_init)
        o_f, o_b = _hgrn(p, lb_all, consts, layer=layer, n_lat=n_lat, n_ctx=n_ctx, tr=256)
        xa = _outproj(x_lat, x_ctx, 0 if last else ctx_row0, mods, p, o_f, o_b, yc_lat, yc_ctx,
                      ln_w3, ln_b3, ws_bf, bs_b, hnorm_w3, w_out_bf, final_w2, layer=layer, n_lat=n_lat,
                      n_rows=n_lat if last else n_rows, tm=tm_fin if last else tm_out, final=last)
        x_lat, x_ctx, ctx_row0 = xa, xa, n_lat
    return xa[None]
```

```python
import functools
import math

import numpy as np
import jax
import jax.numpy as jnp
from jax import lax
from jax.experimental import pallas as pl
from jax.experimental.pallas import tpu as pltpu

F32 = jnp.float32
BF16 = jnp.bfloat16

D_MODEL = 2048
GRID_W = 64
NORM_EPS = 1e-6
D_A = 512
A_HEADS = 4
A_CHUNK = 128
D_B = 512
B_HEADS = 4
B_DIM = 128
D_C = 1024
C_HEADS = 4
C_HEAD_DIM = 128
C_VAL_DIM = 256
ROPE_THETA = 10000.0
ROPE_AXIS_DIM = 64
D_IN = 3 * D_A + 5 * D_B + 4 * D_C
OFF_AU, OFF_AV, OFF_AZ = 0, 512, 1024
OFF_BQ, OFF_BI, OFF_BFF, OFF_BFB, OFF_BZ = 1536, 2048, 2560, 3072, 3584
OFF_CQ, OFF_CK, OFF_CV, OFF_CZ = 4096, 5120, 6144, 7168
Q_SCALE = C_HEAD_DIM ** -0.5 * math.log2(math.e)

LANES = 128
HGRN_CHUNK = 64
HGRN_LEVELS = 6
LOG2_E = math.log2(math.e)
ATTN_ROW_BLOCK = 32
VMEM_LIMIT = 56 * 1024 * 1024


def _cparams(n_axes):
    return pltpu.CompilerParams(
        dimension_semantics=("arbitrary",) * n_axes,
        vmem_limit_bytes=VMEM_LIMIT,
    )


def _silu(v):
    return v * jax.nn.sigmoid(v)


def _nt_dot(a, b):
    return lax.dot_general(a, b, (((1,), (1,)), ((), ())), preferred_element_type=F32)


def _adaln_kernel(c_ref, w_ref, b_ref, o_ref):
    d, tn = w_ref.shape
    ng = tn // LANES

    def body(k, acc):
        rows = pl.ds(pl.multiple_of(k * 8, 8), 8)
        w = w_ref[rows, :]
        out = []
        for r in range(2):
            s = _silu(c_ref[r, rows, :])
            out.append(tuple(acc[r][g] + s * w[:, g * LANES:(g + 1) * LANES] for g in range(ng)))
        return tuple(out)

    zero = tuple(tuple(jnp.zeros((8, LANES), F32) for _ in range(ng)) for _ in range(2))
    acc = lax.fori_loop(0, d // 8, body, zero, unroll=8)
    for r in range(2):
        row = jnp.concatenate([jnp.sum(a, axis=0, keepdims=True) for a in acc[r]], axis=1)
        o_ref[r:r + 1, :] = row + b_ref[...]
    o_ref[2:8, :] = jnp.zeros((6, tn), F32)


def _adaln(c_rep, ada_w, ada_b):
    depth, d, d3 = ada_w.shape
    tn = 1024
    return pl.pallas_call(
        _adaln_kernel,
        grid=(depth, d3 // tn),
        in_specs=[
            pl.BlockSpec((2, d, LANES), lambda l, j: (0, 0, 0)),
            pl.BlockSpec((None, d, tn), lambda l, j: (l, 0, j)),
            pl.BlockSpec((None, 1, tn), lambda l, j: (l, 0, j)),
        ],
        out_specs=pl.BlockSpec((None, 8, tn), lambda l, j: (l, 0, j)),
        out_shape=jax.ShapeDtypeStruct((depth, 8, d3), F32),
        compiler_params=_cparams(2),
    )(c_rep, ada_w, ada_b.reshape(depth, 1, d3))


def _lower_bound_kernel(x_ref, o_ref):
    depth = x_ref.shape[1]
    for d in range(2):
        x = x_ref[d]
        e = jnp.exp(x - jnp.max(x, axis=0, keepdims=True))
        soft = e / jnp.sum(e, axis=0, keepdims=True)
        run = jnp.zeros((1, x.shape[1]), F32)
        o_ref[d, 0:1, :] = run
        for l in range(1, depth):
            run = run + soft[l:l + 1, :]
            o_ref[d, l:l + 1, :] = run


def _lower_bounds(hgrn_lower_bounds):
    return pl.pallas_call(
        _lower_bound_kernel,
        out_shape=jax.ShapeDtypeStruct(hgrn_lower_bounds.shape, F32),
    )(hgrn_lower_bounds.astype(F32))


def _inproj_kernel(x_ref, c_ref, mod_ref, nw_ref, cos_ref, sa_ref, sb_ref, w_ref, o_ref, h_ref,
                   *, n_ctx, tm, tn, sub, n_tiles):
    i = pl.program_id(0)
    j = pl.program_id(1)
    n_col = pl.num_programs(1)
    d = x_ref.shape[1]
    slot = i % 2
    lat_rows_last = tm - n_ctx

    def normalize(dst, tile_is_last):
        nw = nw_ref[...]

        def rows_to_h(src_ref, dst0, n_sub, m):
            w_eff = nw * (1.0 + mod_ref[m:m + 1, d:2 * d])
            shift = mod_ref[m:m + 1, 0:d]
            for r in range(n_sub):
                x = src_ref[r * sub:(r + 1) * sub, :]
                ms = jnp.mean(x * x, axis=-1, keepdims=True)
                h_ref[dst, dst0 + r * sub:dst0 + (r + 1) * sub, :] = (
                    x * lax.rsqrt(ms + NORM_EPS) * w_eff + shift).astype(BF16)

        if tile_is_last:
            rows_to_h(x_ref, 0, lat_rows_last // sub, 0)
            rows_to_h(c_ref, lat_rows_last, n_ctx // sub, 1)
        else:
            rows_to_h(x_ref, 0, tm // sub, 0)

    @pl.when(jnp.logical_and(i == 0, j == 0))
    def _():
        normalize(0, n_tiles == 1)

    is_rope = j * tn == OFF_CQ
    last_col = j == n_col - 1
    halves = [slice(k * D_C, (k + 1) * D_C) for k in range(tn // D_C)]

    def plain_dots():
        h = h_ref[slot]
        for cs in halves:
            o_ref[:, cs] = jnp.dot(h, w_ref[:, cs], preferred_element_type=F32).astype(o_ref.dtype)

    @pl.when(jnp.logical_and(jnp.logical_not(is_rope), jnp.logical_not(last_col)))
    def _():
        plain_dots()

    @pl.when(jnp.logical_and(last_col, i + 1 < n_tiles - 1))
    def _():
        plain_dots()
        normalize(1 - slot, False)

    @pl.when(jnp.logical_and(last_col, i + 1 == n_tiles - 1))
    def _():
        plain_dots()
        normalize(1 - slot, True)

    @pl.when(jnp.logical_and(last_col, i + 1 == n_tiles))
    def _():
        plain_dots()

    @pl.when(is_rope)
    def _():
        h = h_ref[slot]
        for k, cs in enumerate(halves):
            acc = jnp.dot(h, w_ref[:, cs], preferred_element_type=F32)
            mult = Q_SCALE if k == 0 else 1.0
            c = cos_ref[...] * mult
            sa = sa_ref[...] * mult
            sb = sb_ref[...] * mult
            for g in range(D_C // LANES):
                xg = acc[:, g * LANES:(g + 1) * LANES]
                yg = xg * c + pltpu.roll(xg, 32, 1) * sa + pltpu.roll(xg, LANES - 32, 1) * sb
                o_ref[:, k * D_C + g * LANES:k * D_C + (g + 1) * LANES] = yg.astype(o_ref.dtype)


def _inproj(x_lat, x_ctx, ctx_blk, mods, norm_w, cos_t, sa_t, sb_t, w_in_bf, *, layer, n_lat, n_ctx, tm, tn):
    d = x_lat.shape[1]
    d_in = w_in_bf.shape[2]
    n_rows = n_lat + n_ctx
    n_tiles = n_rows // tm
    n_col = d_in // tn
    sub = 32
    assert tn == OFF_CV - OFF_CQ == 2 * D_C and OFF_CQ % tn == 0 and OFF_CQ // tn < n_col - 1
    assert n_rows % tm == 0 and n_ctx <= tm and n_ctx % sub == 0 and tm % sub == 0
    kern = functools.partial(_inproj_kernel, n_ctx=n_ctx, tm=tm, tn=tn, sub=sub, n_tiles=n_tiles)

    def x_tile(i, j):
        return (jnp.minimum(jnp.where(j == n_col - 1, i + 1, i), (n_lat - 1) // tm), 0)

    return pl.pallas_call(
        kern,
        grid=(n_tiles, n_col),
        in_specs=[
            pl.BlockSpec((tm, d), x_tile),
            pl.BlockSpec((n_ctx, d), lambda i, j: (ctx_blk, 0)),
            pl.BlockSpec((None,) + mods.shape[1:], lambda i, j: (layer, 0, 0)),
            pl.BlockSpec((None, 1, d), lambda i, j: (layer, 0, 0)),
            pl.BlockSpec((tm, LANES), lambda i, j: (i, 0)),
            pl.BlockSpec((tm, LANES), lambda i, j: (i, 0)),
            pl.BlockSpec((tm, LANES), lambda i, j: (i, 0)),
            pl.BlockSpec((None, d, tn), lambda i, j: (layer, 0, j)),
        ],
        out_specs=pl.BlockSpec((tm, tn), lambda i, j: (i, j)),
        out_shape=jax.ShapeDtypeStruct((n_rows, d_in), BF16),
        scratch_shapes=[pltpu.VMEM((2, tm, d), BF16)],
        compiler_params=_cparams(2),
    )(x_lat, x_ctx, mods, norm_w, cos_t, sa_t, sb_t, w_in_bf)


def _attn_kernel(lam_ref, sw_ref, q_ref, k_ref, v_ref, z_ref,
                 o_ref, m_ref, l_ref, acc_ref, sa_ref, sb_ref, mpa_ref, mpb_ref, p_ref, al_ref,
                 *, chunks, lam_init, rb):
    tq = q_ref.shape[0]
    m_ref[...] = jnp.full(m_ref.shape, -jnp.inf, F32)
    l_ref[...] = jnp.zeros(l_ref.shape, F32)
    acc_ref[...] = jnp.zeros(acc_ref.shape, F32)
    cw = 2 * LANES

    def scores(row0, size, s_ref, mp_ref):
        for u in range(2):
            q = q_ref[:, u * C_HEAD_DIM:(u + 1) * C_HEAD_DIM]
            mp = None
            for cb in range(size // cw):
                k = k_ref[pl.ds(row0 + cb * cw, cw), u * C_HEAD_DIM:(u + 1) * C_HEAD_DIM]
                sc = _nt_dot(q, k)
                s_ref[u, :, cb * cw:(cb + 1) * cw] = sc
                part = jnp.maximum(sc[:, 0:LANES], sc[:, LANES:cw])
                mp = part if mp is None else jnp.maximum(mp, part)
            mp_ref[u] = mp

    def softmax_pv(row0, size, s_ref, mp_ref):
        v = v_ref[pl.ds(row0, size), :]
        for u in range(2):
            for r0 in range(0, tq, rb):
                r = slice(r0, r0 + rb)
                m_prev = m_ref[u, r, :]
                m_new = jnp.maximum(m_prev, jnp.max(mp_ref[u, r, :], axis=-1, keepdims=True))
                alpha = jnp.exp2(m_prev - m_new)
                p = jnp.exp2((s_ref[u, r, 0:size] - m_new).astype(BF16))
                parts = [p[:, g * LANES:(g + 1) * LANES] for g in range(size // LANES)]
                while len(parts) > 1:
                    parts = [parts[k] + parts[k + 1] for k in range(0, len(parts), 2)]
                l_ref[u, r, :] = alpha * l_ref[u, r, :] + parts[0].astype(F32)
                p_ref[u, r, 0:size] = p
                al_ref[u, r, :] = alpha
                m_ref[u, r, :] = m_new
            acc_ref[u] = al_ref[u] * acc_ref[u] + jnp.dot(p_ref[u, :, 0:size], v, preferred_element_type=F32)

    slots = ((sa_ref, mpa_ref), (sb_ref, mpb_ref))
    row0 = 0
    starts = []
    for size in chunks:
        starts.append(row0)
        row0 += size
    scores(starts[0], chunks[0], *slots[0])
    for c, size in enumerate(chunks):
        if c + 1 < len(chunks):
            scores(starts[c + 1], chunks[c + 1], *slots[(c + 1) % 2])
        softmax_pv(starts[c], size, *slots[c % 2])

    lp = lam_ref[...]
    lam = (jnp.exp(jnp.sum(lp[0:1] * lp[1:2], axis=-1, keepdims=True))
           - jnp.exp(jnp.sum(lp[2:3] * lp[3:4], axis=-1, keepdims=True)) + lam_init)
    l0 = jnp.sum(l_ref[0], axis=-1, keepdims=True)
    l1 = jnp.sum(l_ref[1], axis=-1, keepdims=True)
    o = acc_ref[0] / l0 - lam * (acc_ref[1] / l1)
    ms = jnp.mean(o * o, axis=-1, keepdims=True)
    y = o * lax.rsqrt(ms + NORM_EPS) * sw_ref[...] * (1.0 - lam_init)
    o_ref[...] = (y * _silu(z_ref[...].astype(F32))).astype(o_ref.dtype)


def _key_chunks(n_kv, tk):
    unit = 2 * LANES
    assert n_kv % unit == 0
    sizes = [tk] * (n_kv // tk)
    if n_kv % tk:
        sizes.append(n_kv % tk)
    return tuple(sizes)


def _attention(p, lam_p, subln_w, *, layer, q_row0, n_q, kv_row0, n_kv, tq, tk, lam_init):
    qb0, kb0 = q_row0 // tq, kv_row0 // n_kv
    chunks = _key_chunks(n_kv, tk)
    ts = max(chunks)
    cq, ck, cv, cz = (OFF_CQ // C_VAL_DIM, OFF_CK // C_VAL_DIM, OFF_CV // C_VAL_DIM, OFF_CZ // C_VAL_DIM)
    kern = functools.partial(_attn_kernel, chunks=chunks, lam_init=lam_init, rb=ATTN_ROW_BLOCK)
    in_specs = [
        pl.BlockSpec((None, 4, C_HEAD_DIM), lambda h, i: (layer, 0, 0)),
        pl.BlockSpec((None, 1, C_VAL_DIM), lambda h, i: (layer, 0, 0)),
        pl.BlockSpec((tq, C_VAL_DIM), lambda h, i: (qb0 + i, cq + h)),
        pl.BlockSpec((n_kv, C_VAL_DIM), lambda h, i: (kb0, ck + h)),
        pl.BlockSpec((n_kv, C_VAL_DIM), lambda h, i: (kb0, cv + h)),
        pl.BlockSpec((tq, C_VAL_DIM), lambda h, i: (qb0 + i, cz + h)),
    ]
    return pl.pallas_call(
        kern,
        grid=(C_HEADS, n_q // tq),
        in_specs=in_specs,
        out_specs=pl.BlockSpec((tq, C_VAL_DIM), lambda h, i: (i, h)),
        out_shape=jax.ShapeDtypeStruct((n_q, D_C), BF16),
        scratch_shapes=[
            pltpu.VMEM((2, tq, 1), F32),
            pltpu.VMEM((2, tq, LANES), F32),
            pltpu.VMEM((2, tq, C_VAL_DIM), F32),
            pltpu.VMEM((2, tq, ts), F32),
            pltpu.VMEM((2, tq, ts), F32),
            pltpu.VMEM((2, tq, LANES), F32),
            pltpu.VMEM((2, tq, LANES), F32),
            pltpu.VMEM((2, tq, ts), BF16),
            pltpu.VMEM((2, tq, 1), F32),
        ],
        compiler_params=_cparams(2),
    )(lam_p, subln_w, p, p, p, p)


def _hgrn_constants():
    c = HGRN_CHUNK
    t = np.arange(c)
    mats = [(t[None, :] <= t[:, None]), (t[None, :] > t[:, None])]
    masks, rts = [], []
    for lvl in range(HGRN_LEVELS):
        n = (c // 2) >> lvl
        later = (t & n) != 0
        parent = t // (2 * n)
        mid = parent * 2 * n + n - 1
        dm = np.zeros((c, c), bool)
        for tt in range(c):
            if later[tt]:
                dm[tt, mid[tt] + 1:tt + 1] = True
            else:
                dm[tt, tt + 1:mid[tt] + 1] = True
        mats.append(dm)
        masks.append(later[:, None] & (~later[None, :]) & (parent[:, None] == parent[None, :]))
        rts.append(later)
    masks.append(np.eye(c, dtype=bool))
    cm = np.concatenate(mats, axis=0).astype(np.float32)
    mk = np.stack(masks).astype(np.float32)
    rt = np.repeat(np.stack(rts).astype(np.float32)[:, :, None], LANES, axis=2)

    def flip(a):
        blocks = a.reshape(-1, c, a.shape[-1])
        if a.shape[-1] == c:
            blocks = blocks[:, ::-1, ::-1]
        else:
            blocks = blocks[:, ::-1, :]
        return blocks.reshape(a.shape)

    cm2 = np.stack([cm, flip(cm)])
    cm2 = np.concatenate([cm2, cm2, cm2], axis=2)
    mk2 =np.stack([mk, flip(mk.reshape(-1, c)).reshape(mk.shape)])
    rt2 = np.stack([rt, flip(rt.reshape(-1, LANES)).reshape(rt.shape)])
    return cm2, mk2, rt2


def _hgrn_kernel(lbf_ref, lbb_ref, cm_ref, mk_ref, rt_ref,
                 qf_ref, vf_ref, af_ref, qb_ref, vb_ref, ab_ref,
                 of_ref, ob_ref, st_ref, f_ref, qs_ref, kk_ref, sc_ref, *, n_sub):
    c = HGRN_CHUNK

    @pl.when(pl.program_id(0) == 0)
    def _():
        st_ref[...] = jnp.zeros(st_ref.shape, F32)

    dirs = ((0, af_ref, qf_ref, vf_ref, lbf_ref, of_ref), (1, ab_ref, qb_ref, vb_ref, lbb_ref, ob_ref))

    def chunk_rows(d, t):
        blk = t if d == 0 else n_sub - 1 - t
        return slice(blk * c, (blk + 1) * c)

    for d, a_ref, q_ref, v_ref, lb_ref, o_ref in dirs:
        a = a_ref[...].astype(F32)
        lb = lb_ref[...]
        log_lb = jnp.log(lb)
        log_1m = jnp.log1p(-lb)
        log_sig = jnp.minimum(a, 0.0) - jnp.log(1.0 + jnp.exp(-jnp.abs(a)))
        q2 = log_1m + log_sig
        g = jnp.maximum(log_lb, q2) + jnp.log(1.0 + jnp.exp(-jnp.abs(log_lb - q2)))
        g = g * LOG2_E
        kk_ref[d] = (1.0 - lb) * jax.nn.sigmoid(-a)
        qs_ref[d] = _silu(q_ref[...].astype(F32))
        g1 = g.astype(BF16)
        r1 = g - g1.astype(F32)
        g2 = r1.astype(BF16)
        g3 = (r1 - g2.astype(F32)).astype(BF16)
        for t in range(n_sub):
            r = chunk_rows(d, t)
            gcat = jnp.concatenate([g1[r], g2[r], g3[r]], axis=0)
            f_ref[d, t] = jnp.exp2(jnp.dot(cm_ref[d], gcat, preferred_element_type=F32))

    for d, a_ref, q_ref, v_ref, lb_ref, o_ref in dirs:
        q_side = [rt_ref[d, lvl] != 0.0 for lvl in range(HGRN_LEVELS)]
        for t in range(n_sub):
            r = chunk_rows(d, t)
            for h in range(B_HEADS):
                ln = slice(h * B_DIM, (h + 1) * B_DIM)
                qs = qs_ref[d, r, ln]
                kk = kk_ref[d, r, ln]
                sc = mk_ref[d, HGRN_LEVELS] * _nt_dot(qs.astype(BF16), kk.astype(BF16))
                for lvl in range(HGRN_LEVELS):
                    fl = f_ref[d, t, (2 + lvl) * c:(3 + lvl) * c, ln]
                    xl = (jnp.where(q_side[lvl], qs, kk) * fl).astype(BF16)
                    sc = sc + mk_ref[d, lvl] * _nt_dot(xl, xl)
                sc_ref[d, t, h] = sc.astype(BF16)

    for t in range(n_sub):
        for d, a_ref, q_ref, v_ref, lb_ref, o_ref in dirs:
            r = chunk_rows(d, t)
            last = c - 1 if d == 0 else 0
            for h in range(B_HEADS):
                ln = slice(h * B_DIM, (h + 1) * B_DIM)
                v32 = v_ref[r, ln].astype(F32)
                qe = (qs_ref[d, r, ln] * f_ref[d, t, 0:c, ln]).astype(BF16)
                kdec = (kk_ref[d, r, ln] * f_ref[d, t, c:2 * c, ln]).astype(BF16)
                e_last = f_ref[d, t, last:last + 1, ln]
                st = st_ref[d, h]
                o = (jnp.dot(sc_ref[d, t, h], v32.astype(BF16), preferred_element_type=F32)
                     + _nt_dot(qe, st.astype(BF16)))
                st_ref[d, h] = st * e_last + jnp.dot(v32.T.astype(BF16), kdec, preferred_element_type=F32)
                o_ref[r, ln] = o.astype(o_ref.dtype)


def _hgrn(p, lb_all, consts, *, layer, n_lat, n_ctx, tr):
    rows = p.shape[0]
    cm, mk, rt = consts
    n_lat_b, n_ctx_b = n_lat // tr, n_ctx // tr
    nb = n_lat_b + n_ctx_b
    n_sub = tr // HGRN_CHUNK

    def fwd_blk(s):
        return jnp.where(s < n_ctx_b, n_lat_b + s, s - n_ctx_b)

    def bwd_blk(s):
        return nb - 1 - s

    cq, ci, cff, cfb = OFF_BQ // D_B, OFF_BI // D_B, OFF_BFF // D_B, OFF_BFB // D_B
    full = lambda a: pl.BlockSpec(a.shape, lambda s: (0,) * a.ndim)
    kern = functools.partial(_hgrn_kernel, n_sub=n_sub)
    return pl.pallas_call(
        kern,
        grid=(nb,),
        in_specs=[
            pl.BlockSpec((None, None, 1, D_B), lambda s: (0, layer, 0, 0)),
            pl.BlockSpec((None, None, 1, D_B), lambda s: (1, layer, 0, 0)),
            full(cm), full(mk), full(rt),
            pl.BlockSpec((tr, D_B), lambda s: (fwd_blk(s), cq)),
            pl.BlockSpec((tr, D_B), lambda s: (fwd_blk(s), ci)),
            pl.BlockSpec((tr, D_B), lambda s: (fwd_blk(s), cff)),
            pl.BlockSpec((tr, D_B), lambda s: (bwd_blk(s), cq)),
            pl.BlockSpec((tr, D_B), lambda s: (bwd_blk(s), ci)),
            pl.BlockSpec((tr, D_B), lambda s: (bwd_blk(s), cfb)),
        ],
        out_specs=[
            pl.BlockSpec((tr, D_B), lambda s: (fwd_blk(s), 0)),
            pl.BlockSpec((tr, D_B), lambda s: (bwd_blk(s), 0)),
        ],
        out_shape=[jax.ShapeDtypeStruct((rows, D_B), F32)] * 2,
        scratch_shapes=[
            pltpu.VMEM((2, B_HEADS, B_DIM, B_DIM), F32),
            pltpu.VMEM((2, n_sub, (2 + HGRN_LEVELS) * HGRN_CHUNK, D_B), F32),
            pltpu.VMEM((2, tr, D_B), F32),
            pltpu.VMEM((2, tr, D_B), F32),
            pltpu.VMEM((2, n_sub, B_HEADS, HGRN_CHUNK, HGRN_CHUNK), BF16),
        ],
        compiler_params=_cparams(1),
    )(lb_all, lb_all, cm, mk, rt, p, p, p, p, p, p)


def _gelu(v):
    return 0.5 * v * (1.0 + lax.erf(v * (2.0 ** -0.5)))


def _out_kernel(x_ref, xc_ref, mod_ref, u_ref, v_ref, z_ref, bz_ref, of_ref, ob_ref, ycl_ref, ycc_ref,
                lnw_ref, lnb_ref, ws_ref, bs_ref, hw_ref, wout_ref, fw_ref,
                o_ref, y_ref, *, n_lat, tm, final):
    i = pl.program_id(0)
    d = x_ref.shape[1]
    n_lat_tiles = n_lat // tm

    def mix_chunk(n):
        rows = pl.ds(n * A_CHUNK, A_CHUNK)
        u = _gelu(u_ref[rows, :].astype(F32))
        v = _gelu(v_ref[rows, :].astype(F32))
        mu = jnp.mean(v, axis=-1, keepdims=True)
        vc = v - mu
        var = jnp.mean(vc * vc, axis=-1, keepdims=True)
        vln = (vc * lax.rsqrt(var + NORM_EPS) * lnw_ref[...] + lnb_ref[...]).astype(BF16)
        gz = _silu(z_ref[rows, :].astype(F32))
        ob = of_ref[rows, :] + ob_ref[rows, :]
        gbz = _silu(bz_ref[rows, :].astype(F32))
        for h in range(A_HEADS):
            ln = slice(h * LANES, (h + 1) * LANES)
            s = jnp.dot(ws_ref[h], vln[:, ln], preferred_element_type=F32) + bs_ref[h]
            y_ref[rows, h * LANES:(h + 1) * LANES] = (u[:, ln] * s * gz[:, ln]).astype(BF16)
            oh = ob[:, ln]
            ms = jnp.mean(oh * oh, axis=-1, keepdims=True)
            yb = oh * lax.rsqrt(ms + NORM_EPS) * hw_ref[...] * gbz[:, ln]
            y_ref[rows, D_A + h * LANES:D_A + (h + 1) * LANES] = yb.astype(BF16)

    yc = jnp.where(i < n_lat_tiles, ycl_ref[...], ycc_ref[...])
    out = jnp.dot(yc, wout_ref[D_A + D_B:, :], preferred_element_type=F32)
    for n in range(tm // A_CHUNK):
        mix_chunk(n)
    out = out + jnp.dot(y_ref[...], wout_ref[0:D_A + D_B, :], preferred_element_type=F32)
    gate = jnp.where(i >= n_lat_tiles, mod_ref[1:2, 2 * d:3 * d], mod_ref[0:1, 2 * d:3 * d])
    xn = jnp.where(i < n_lat_tiles, x_ref[...], xc_ref[...]) + gate * out
    if final:
        ms = jnp.mean(xn * xn, axis=-1, keepdims=True)
        xn = xn * lax.rsqrt(ms + NORM_EPS) * fw_ref[...]
    o_ref[...] = xn


def _outproj(x_lat, x_ctx, ctx_row0, mods, p, o_f, o_b, yc_lat, yc_ctx, ln_w, ln_b, ws_bf, bs_b, hnorm_w,
             w_out_bf, final_w, *, layer, n_lat, n_rows, tm, final):
    d = x_lat.shape[1]
    assert ctx_row0 % tm == 0
    ctx_blk0 = ctx_row0 // tm
    assert n_lat % tm == 0 and n_rows % tm == 0 and yc_ctx.shape[0] % tm == 0
    n_lat_tiles = n_lat // tm
    kern = functools.partial(_out_kernel, n_lat=n_lat, tm=tm, final=final)
    cu, cv, cz, cbz = OFF_AU // D_A, OFF_AV // D_A, OFF_AZ // D_A, OFF_BZ // D_B
    full = lambda a: pl.BlockSpec(a.shape, lambda i: (0,) * a.ndim)
    layered = lambda a: pl.BlockSpec((None,) + a.shape[1:], lambda i: (layer,) + (0,) * (a.ndim - 1))
    return pl.pallas_call(
        kern,
        grid=(n_rows // tm,),
        in_specs=[
            pl.BlockSpec((tm, d), lambda i: (jnp.minimum(i, n_lat_tiles - 1), 0)),
            pl.BlockSpec((tm, d), lambda i: (ctx_blk0 + jnp.maximum(i - n_lat_tiles, 0), 0)),
            layered(mods),
            pl.BlockSpec((tm, D_A), lambda i: (i, cu)),
            pl.BlockSpec((tm, D_A), lambda i: (i, cv)),
            pl.BlockSpec((tm, D_A), lambda i: (i, cz)),
            pl.BlockSpec((tm, D_B), lambda i: (i, cbz)),
            pl.BlockSpec((tm, D_B), lambda i: (i, 0)),
            pl.BlockSpec((tm, D_B), lambda i: (i, 0)),
            pl.BlockSpec((tm, D_C), lambda i: (jnp.minimum(i, n_lat_tiles - 1), 0)),
            pl.BlockSpec((tm, D_C), lambda i: (jnp.maximum(i - n_lat_tiles, 0), 0)),
            layered(ln_w), layered(ln_b), layered(ws_bf), layered(bs_b), layered(hnorm_w), layered(w_out_bf),
            full(final_w),
        ],
        out_specs=pl.BlockSpec((tm, d), lambda i: (i, 0)),
        out_shape=jax.ShapeDtypeStruct((n_rows, d), F32),
        scratch_shapes=[pltpu.VMEM((tm, D_A + D_B), BF16)],
        compiler_params=_cparams(1),
    )(x_lat, x_ctx, mods, p, p, p, p, o_f, o_b, yc_lat, yc_ctx, ln_w, ln_b, ws_bf, bs_b, hnorm_w, w_out_bf, final_w)


def _rope_tables(n_lat, n_ctx):
    rows = n_lat // GRID_W
    inv = (ROPE_THETA ** (-np.arange(0, ROPE_AXIS_DIM, 2, dtype=np.float32) / ROPE_AXIS_DIM)).astype(np.float32)
    ang_r = np.arange(rows, dtype=np.float32)[:, None] * inv[None, :]
    ang_c = np.arange(GRID_W, dtype=np.float32)[:, None] * inv[None, :]
    per_row = lambda a: jnp.repeat(jnp.asarray(a, F32), GRID_W, axis=0)
    per_col = lambda a: jnp.tile(jnp.asarray(a, F32), (rows, 1))
    cos_r, sin_r = per_row(np.cos(ang_r)), per_row(np.sin(ang_r))
    cos_c, sin_c = per_col(np.cos(ang_c)), per_col(np.sin(ang_c))
    zero = jnp.zeros_like(cos_r)
    cos_t = jnp.concatenate([cos_r, cos_r, cos_c, cos_c], axis=1)
    sa_t = jnp.concatenate([zero, sin_r, zero, sin_c], axis=1)
    sb_t = jnp.concatenate([-sin_r, zero, -sin_c, zero], axis=1)
    pad = lambda a, v: jnp.concatenate([a, jnp.full((n_ctx, LANES), v, F32)], axis=0)
    return pad(cos_t, 1.0), pad(sa_t, 0.0), pad(sb_t, 0.0)


def _pick_tile(n, candidates):
    for t in candidates:
        if n % t == 0:
            return t
    raise ValueError(f"no tile for {n}")


def kernel(x, c, ctx, c_ctx, ada_w, ada_b, norm_w, w_in, gmlp_ln_w, gmlp_ln_b, gmlp_ws, gmlp_bs,
           hgrn_lower_bounds, hgrn_norm_w, diff_lambda, diff_subln_w, w_out, final_norm_w):
    bsz, n_lat, d = x.shape
    n_ctx = ctx.shape[1]
    depth = ada_w.shape[0]
    assert bsz == 1 and d == D_MODEL
    assert n_lat % 512 == 0 and n_ctx % 256 == 0 and n_lat % GRID_W == 0
    n_rows = n_lat + n_ctx

    x_lat, x_ctx, ctx_row0 = x[0], ctx[0], 0
    c_rep = jnp.broadcast_to(jnp.stack([c[0], c_ctx])[:, :, None], (2, d, LANES))
    mods = _adaln(c_rep, ada_w, ada_b)
    lb_all = _lower_bounds(hgrn_lower_bounds).reshape(2, depth, 1, D_B)
    cos_t, sa_t, sb_t = _rope_tables(n_lat, n_ctx)
    cm, mk, rt = _hgrn_constants()
    consts = (jnp.asarray(cm, BF16), jnp.asarray(mk), jnp.asarray(rt))

    w_in_bf = w_in.astype(BF16)
    w_out_bf = w_out.astype(BF16)
    ws_bf = gmlp_ws.astype(BF16)
    bs_b = jnp.broadcast_to(gmlp_bs[:, :, :, None], gmlp_bs.shape + (LANES,))
    norm_w3 = norm_w.reshape(depth, 1, d)
    ln_w3 = gmlp_ln_w.reshape(depth, 1, D_A)
    ln_b3 = gmlp_ln_b.reshape(depth, 1, D_A)
    hnorm_w3 = hgrn_norm_w.reshape(depth, 1, B_DIM)
    subln_w3 = diff_subln_w.reshape(depth, 1, C_VAL_DIM)
    final_w2 = final_norm_w.reshape(1, d)

    tm_in = _pick_tile(n_rows, (768, 256))
    tm_out = _pick_tile(n_ctx, (256,))
    tm_fin = _pick_tile(n_lat, (512,))
    tq = _pick_tile(n_lat, (512,))
    tk = 2048 if n_lat >= 8192 else 512

    for layer in range(depth):
        lam_init = 0.8 - 0.6 * math.exp(-0.3 * layer)
        last = layer == depth - 1
        p = _inproj(x_lat, x_ctx, ctx_row0 // n_ctx, mods, norm_w3, cos_t, sa_t, sb_t, w_in_bf,
                    layer=layer, n_lat=n_lat, n_ctx=n_ctx, tm=tm_in, tn=2 * D_C)
        yc_lat = _attention(p, diff_lambda, subln_w3, layer=layer, q_row0=0, n_q=n_lat,
                            kv_row0=0, n_kv=n_rows, tq=tq, tk=tk, lam_init=lam_init)
        if last:
            yc_ctx = yc_lat
        else:
            yc_ctx = _attention(p, diff_lambda, subln_w3, layer=layer, q_row0=n_lat, n_q=n_ctx,
                                kv_row0=n_lat, n_kv=n_ctx, tq=n_ctx, tk=n_ctx, lam_init=lam_init)
        o_f, o_b = _hgrn(p, lb_all, consts, layer=layer, n_lat=n_lat, n_ctx=n_ctx, tr=256)
        xa = _outproj(x_lat, x_ctx, 0 if last else ctx_row0, mods, p, o_f, o_b, yc_lat, yc_ctx,
                      ln_w3, ln_b3, ws_bf, bs_b, hnorm_w3, w_out_bf, final_w2, layer=layer, n_lat=n_lat,
                      n_rows=n_lat if last else n_rows, tm=tm_fin if last else tm_out, final=last)
        x_lat, x_ctx, ctx_row0 = xa, xa, n_lat
    return xa[None]
```
